```python
import math
import jax, jax.numpy as jnp
from jax import lax
import numpy as np

D_MODEL = 2048
BATCH = 32
SEQ = 256
DEPTH = 2
DEC_BATCH = 2
DEC_SEQ = 2048
PAST_LEN = 256

GRID_W = 64
D_MIX = D_MODEL
ATT_HEADS = 8
ATT_KV_HEADS = 2
GQA_GROUP = ATT_HEADS // ATT_KV_HEADS
HEAD_DIM = 128
ATT_WIDTH = ATT_HEADS * HEAD_DIM
KV_WIDTH = ATT_KV_HEADS * HEAD_DIM
WINDOW = 128
ATT_BLOCK = 128
ROPE_BASE = 10000.0
HG_HEADS = 4
HG_DK = 128
HG_DV = 128
HG_KW = HG_HEADS * HG_DK
HG_WIDTH = HG_HEADS * HG_DV
HG_CHUNK = 32
GATE_EPS = 1e-6
CV_WIDTH = D_MIX - ATT_WIDTH - HG_WIDTH
CONV_TAPS = 31
SPLITS = (ATT_WIDTH, KV_WIDTH, KV_WIDTH, ATT_WIDTH,
          HG_KW, HG_WIDTH, HG_KW, HG_KW, HG_WIDTH,
          CV_WIDTH, CV_WIDTH, CV_WIDTH)
SPLIT_POINTS = tuple(int(s) for s in np.cumsum(SPLITS)[:-1])
D_IN = int(sum(SPLITS))
ALPHA = float((2 * DEPTH) ** 0.25)
BETA = float((8 * DEPTH) ** -0.25)
NEG = -1e30

kernel_name = "hymba_hgrn2_swa_conformer_dit_step"

F32 = jnp.float32


def layer_norm(x, w, b, eps=1e-5):
    xf = x.astype(F32)
    mu = jnp.mean(xf, axis=-1, keepdims=True)
    var = jnp.mean(jnp.square(xf - mu), axis=-1, keepdims=True)
    return ((xf - mu) * lax.rsqrt(var + eps) * w.astype(F32) + b.astype(F32)).astype(x.dtype)


def rms_norm(x, w, eps=1e-6):
    xf = x.astype(F32)
    return xf * lax.rsqrt(jnp.mean(xf * xf, axis=-1, keepdims=True) + eps) * w.astype(F32)


def axial_rope(x):
    seq_len = x.shape[1]
    rows = seq_len // GRID_W
    row = jnp.broadcast_to(jnp.arange(rows)[:, None], (rows, GRID_W)).reshape(-1)
    col = jnp.broadcast_to(jnp.arange(GRID_W)[None, :], (rows, GRID_W)).reshape(-1)
    half = HEAD_DIM // 2
    nf = half // 2
    inv = ROPE_BASE ** (-jnp.arange(nf, dtype=F32) / nf)

    def rot(xp, pos):
        ang = pos.astype(F32)[:, None] * inv[None, :]
        cos = jnp.cos(ang)[None, :, None, :]
        sin = jnp.sin(ang)[None, :, None, :]
        x1 = xp[..., :nf].astype(F32)
        x2 = xp[..., nf:].astype(F32)
        return jnp.concatenate([x1 * cos - x2 * sin, x2 * cos + x1 * sin], axis=-1)

    return jnp.concatenate([rot(x[..., :half], row), rot(x[..., half:], col)], axis=-1).astype(x.dtype)


def sink_softmax(logits, sink):
    s = jnp.broadcast_to(sink[:, :, None, None], logits.shape[:-1] + (1,))
    p = jax.nn.softmax(jnp.concatenate([logits, s], axis=-1), axis=-1)
    return p[..., :-1]


def context_attention(q, k, v, sink):
    B, L, H, D = q.shape
    nq = L // ATT_BLOCK
    scale = D ** -0.5
    qb = q.reshape(B, nq, ATT_BLOCK, ATT_KV_HEADS, GQA_GROUP, D).transpose(1, 0, 2, 3, 4, 5)

    def one_block(qblk):
        s = jnp.einsum('bqkgd,bckd->bkgqc', qblk, k).astype(F32) * scale
        p = sink_softmax(s, sink).astype(v.dtype)
        return jnp.einsum('bkgqc,bckd->bqkgd', p, v)

    o = lax.map(one_block, qb)
    return o.transpose(1, 0, 2, 3, 4, 5).reshape(B, L, H * D)


def latent_attention(q, k, v, ck, cv, sink):
    B, S, H, D = q.shape
    nb = S // ATT_BLOCK
    scale = D ** -0.5
    qb = q.reshape(B, nb, ATT_BLOCK, ATT_KV_HEADS, GQA_GROUP, D)
    pad = ((0, 0), (ATT_BLOCK, ATT_BLOCK), (0, 0), (0, 0))
    kb = jnp.pad(k, pad).reshape(B, nb + 2, ATT_BLOCK, ATT_KV_HEADS, D)
    vb = jnp.pad(v, pad).reshape(B, nb + 2, ATT_BLOCK, ATT_KV_HEADS, D)
    kl = jnp.concatenate([kb[:, :nb], kb[:, 1:nb + 1], kb[:, 2:]], axis=2)
    vl = jnp.concatenate([vb[:, :nb], vb[:, 1:nb + 1], vb[:, 2:]], axis=2)
    qpos = jnp.arange(S).reshape(nb, ATT_BLOCK)
    kpos = jnp.arange(nb)[:, None] * ATT_BLOCK - ATT_BLOCK + jnp.arange(3 * ATT_BLOCK)[None, :]
    mask = ((jnp.abs(qpos[:, :, None] - kpos[:, None, :]) <= WINDOW)
            & (kpos[:, None, :] >= 0) & (kpos[:, None, :] < S))
    s_loc = jnp.einsum('bnqkgd,bnrkd->bnkgqr', qb, kl).astype(F32) * scale
    s_loc = jnp.where(mask[None, :, None, None, :, :], s_loc, NEG)
    s_ctx = jnp.einsum('bnqkgd,bckd->bnkgqc', qb, ck.astype(q.dtype)).astype(F32) * scale
    p = sink_softmax(jnp.concatenate([s_loc, s_ctx], axis=-1), sink).astype(v.dtype)
    r = 3 * ATT_BLOCK
    o = (jnp.einsum('bnkgqr,bnrkd->bnqkgd', p[..., :r], vl)
         + jnp.einsum('bnkgqc,bckd->bnqkgd', p[..., r:], cv.astype(v.dtype)))
    return o.reshape(B, S, H * D)


def hgrn2_chunk_scan(q, k, v, logf, s0):
    B, L, H, DK = q.shape
    DV = v.shape[-1]
    n = L // HG_CHUNK

    def chunks(t):
        return t.reshape(B, n, HG_CHUNK, H, t.shape[-1]).transpose(1, 0, 3, 2, 4)

    causal = jnp.tril(jnp.ones((HG_CHUNK, HG_CHUNK), dtype=bool))

    def step(S, inp):
        qc, kc, vc, gc = inp
        G = jnp.cumsum(gc, axis=2)
        inter = jnp.einsum('bhtd,bhde->bhte', qc * jnp.exp(G), S)
        diff = jnp.where(causal[None, None, :, :, None],
                         G[:, :, :, None, :] - G[:, :, None, :, :], NEG)
        A = jnp.einsum('bhtd,bhtsd,bhsd->bhts', qc, jnp.exp(diff), kc)
        intra = jnp.einsum('bhts,bhse->bhte', A, vc)
        g_last = G[:, :, -1:, :]
        S_new = (jnp.exp(g_last[:, :, 0, :])[..., None] * S
                 + jnp.einsum('bhsd,bhse->bhde', kc * jnp.exp(g_last - G), vc))
        return S_new, inter + intra

    s_fin, o = lax.scan(step, s0.astype(F32), (chunks(q), chunks(k), chunks(v), chunks(logf)))
    o = o.transpose(1, 0, 3, 2, 4).reshape(B, L, H, DV)
    return o, s_fin


def hgrn2_branch(hq, hi, hf_f, hf_b, lb_f, lb_b, norm_w, s0_f, s0_b):
    B, L, _ = hq.shape
    q = jax.nn.silu(hq.astype(F32)).reshape(B, L, HG_HEADS, HG_DK) * (HG_DK ** -0.5)
    v = hi.astype(F32).reshape(B, L, HG_HEADS, HG_DV)

    def gates(hf, lb):
        lb = lb.astype(F32).reshape(HG_HEADS, HG_DK)
        one_minus_f = (1.0 - lb) * jax.nn.sigmoid(-hf.astype(F32).reshape(B, L, HG_HEADS, HG_DK))
        logf = jnp.log1p(-jnp.minimum(one_minus_f, 1.0 - GATE_EPS))
        return one_minus_f, logf

    k_f, g_f = gates(hf_f, lb_f)
    k_b, g_b = gates(hf_b, lb_b)
    o_f, s_f = hgrn2_chunk_scan(q, k_f, v, g_f, s0_f)
    o_b, s_b = hgrn2_chunk_scan(q[:, ::-1], k_b[:, ::-1], v[:, ::-1], g_b[:, ::-1], s0_b)
    o = o_f + o_b[:, ::-1]
    o = rms_norm(o, norm_w.reshape(HG_HEADS, HG_DV))
    return o.reshape(B, L, HG_WIDTH).astype(hq.dtype), jnp.stack([s_f, s_b], axis=1)


def conv_branch(ha, hb, w, b, ln_w, ln_b):
    u = ha * jax.nn.sigmoid(hb)
    y = lax.conv_general_dilated(u, w[:, None, :].astype(u.dtype), window_strides=(1,),
                                 padding=[(CONV_TAPS // 2, CONV_TAPS // 2)],
                                 dimension_numbers=('NWC', 'WIO', 'NWC'),
                                 feature_group_count=CV_WIDTH) + b.astype(u.dtype)
    return jax.nn.silu(layer_norm(y, ln_w, ln_b))


def modulation(cond, ada_w_l, ada_b_l):
    m = jax.nn.silu(cond) @ ada_w_l + ada_b_l
    return jnp.split(m, 3, axis=-1)


def trunk_layer(x, shift, scale, gate, w_in_l, w_out_l, ln_w_l, ln_b_l, sink_l, lb_l, hg_norm_l,
                conv_w_l, conv_b_l, conv_ln_w_l, conv_ln_b_l, ctx):
    B, L, _ = x.shape
    h = x * (1 + scale) + shift
    z = h @ w_in_l
    aq, ak, av, ag, hq, hi, hff, hfb, hgate, ca, cb, cgate = jnp.split(z, SPLIT_POINTS, axis=-1)
    q = aq.reshape(B, L, ATT_HEADS, HEAD_DIM)
    k = ak.reshape(B, L, ATT_KV_HEADS, HEAD_DIM)
    v = av.reshape(B, L, ATT_KV_HEADS, HEAD_DIM)
    sink = sink_l.astype(F32).reshape(ATT_KV_HEADS, GQA_GROUP)
    if ctx is None:
        att = context_attention(q, k, v, sink)
        s0 = jnp.zeros((B, 2, HG_HEADS, HG_DK, HG_DV), F32)
    else:
        ctx_k, ctx_v, s0 = ctx
        att = latent_attention(axial_rope(q), axial_rope(k), v, ctx_k, ctx_v, sink)
    hg_out, s_fin = hgrn2_branch(hq, hi, hff, hfb, lb_l[0], lb_l[1], hg_norm_l, s0[:, 0], s0[:, 1])
    cv_out = conv_branch(ca, cb, conv_w_l, conv_b_l, conv_ln_w_l, conv_ln_b_l)
    mix = jnp.concatenate([att * jax.nn.silu(ag), hg_out * jax.nn.silu(hgate),
                           cv_out * jax.nn.silu(cgate)], axis=-1)
    out = mix @ w_out_l
    x_new = layer_norm(ALPHA * x + gate * out, ln_w_l, ln_b_l)
    return x_new, k, v, s_fin.astype(x.dtype)


def setup_inputs(seed: int = 0) -> dict:
    key = jax.random.key(seed)
    ks = jax.random.split(key, 20)

    def nrm(k, shape, s):
        return s * jax.random.normal(k, shape, F32)

    col_scales = (1.0, 1.0, BETA, 1.0, 1.0, BETA, 1.0, 1.0, 1.0, BETA, 1.0, 1.0)
    col_scale = jnp.concatenate([jnp.full((n,), s, F32) for n, s in zip(SPLITS, col_scales)])
    return {
        "x_prompt": nrm(ks[0], (BATCH, SEQ, D_MODEL), 1.0),
        "x_sample": nrm(ks[1], (DEC_BATCH, DEC_SEQ, D_MODEL), 1.0),
        "cache_k": nrm(ks[2], (DEC_BATCH, DEPTH, PAST_LEN, ATT_KV_HEADS, HEAD_DIM), 1.0),
        "cache_v": nrm(ks[3], (DEC_BATCH, DEPTH, PAST_LEN, ATT_KV_HEADS, HEAD_DIM), 1.0),
        "state_hgrn": nrm(ks[4], (DEC_BATCH, DEPTH, 2, HG_HEADS, HG_DK, HG_DV), 0.5),
        "c": nrm(ks[5], (DEC_BATCH, D_MODEL), 1.0),
        "c_ctx": nrm(ks[6], (D_MODEL,), 1.0),
        "ada_w": nrm(ks[7], (DEPTH, D_MODEL, 3 * D_MODEL), 0.5 * D_MODEL ** -0.5),
        "ada_b": nrm(ks[8], (DEPTH, 3 * D_MODEL), 0.02),
        "w_in": nrm(ks[9], (DEPTH, D_MODEL, D_IN), D_MODEL ** -0.5) * col_scale,
        "w_out": nrm(ks[10], (DEPTH, D_MIX, D_MODEL), BETA * D_MIX ** -0.5),
        "attn_sink": nrm(ks[11], (DEPTH, ATT_HEADS), 0.5),
        "hg_lower_bounds": nrm(ks[12], (2, DEPTH, HG_KW), 0.5),
        "hg_norm_w": 1.0 + nrm(ks[13], (DEPTH, HG_WIDTH), 0.02),
        "conv_w": nrm(ks[14], (DEPTH, CONV_TAPS, CV_WIDTH), CONV_TAPS ** -0.5),
        "conv_b": nrm(ks[15], (DEPTH, CV_WIDTH), 0.02),
        "conv_ln_w": 1.0 + nrm(ks[16], (DEPTH, CV_WIDTH), 0.02),
        "conv_ln_b": nrm(ks[17], (DEPTH, CV_WIDTH), 0.02),
        "ln_w": 1.0 + nrm(ks[18], (DEPTH, D_MODEL), 0.02),
        "ln_b": nrm(ks[19], (DEPTH, D_MODEL), 0.02),
    }


def reference(x_prompt, x_sample, cache_k, cache_v, state_hgrn, c, c_ctx, ada_w, ada_b, w_in, w_out,
              attn_sink, hg_lower_bounds, hg_norm_w, conv_w, conv_b, conv_ln_w, conv_ln_b, ln_w, ln_b):
    lbs = jax.nn.softmax(hg_lower_bounds.astype(F32), axis=1)
    lbs = jnp.cumsum(lbs, axis=1) - lbs[:, :1]

    y_prompt = x_prompt
    ks_, vs_, ss_ = [], [], []
    for l in range(DEPTH):
        shift, scale, gate = modulation(c_ctx, ada_w[l], ada_b[l])
        y_prompt, k_l, v_l, s_l = trunk_layer(
            y_prompt, shift, scale, gate, w_in[l], w_out[l], ln_w[l], ln_b[l], attn_sink[l], lbs[:, l],
            hg_norm_w[l], conv_w[l], conv_b[l], conv_ln_w[l], conv_ln_b[l], None)
        ks_.append(k_l)
        vs_.append(v_l)
        ss_.append(s_l)
    new_cache_k = jnp.stack(ks_, axis=1)
    new_cache_v = jnp.stack(vs_, axis=1)
    new_state_hgrn = jnp.stack(ss_, axis=1)

    y_sample = x_sample
    for l in range(DEPTH):
        shift, scale, gate = modulation(c, ada_w[l], ada_b[l])
        y_sample, _, _, _ = trunk_layer(
            y_sample, shift[:, None, :], scale[:, None, :], gate[:, None, :], w_in[l], w_out[l],
            ln_w[l], ln_b[l], attn_sink[l], lbs[:, l], hg_norm_w[l], conv_w[l], conv_b[l],
            conv_ln_w[l], conv_ln_b[l], (cache_k[:, l], cache_v[:, l], state_hgrn[:, l]))

    return (y_prompt, y_sample, new_cache_k, new_cache_v, new_state_hgrn)
```

```python
import functools

import numpy as np
import jax
import jax.numpy as jnp
from jax import lax
from jax.experimental import pallas as pl
from jax.experimental.pallas import tpu as pltpu

F32 = jnp.float32
BF16 = jnp.bfloat16

ATT_HEADS = 8
ATT_KV_HEADS = 2
GQA_GROUP = ATT_HEADS // ATT_KV_HEADS
HEAD_DIM = 128
ATT_WIDTH = ATT_HEADS * HEAD_DIM
KV_WIDTH = ATT_KV_HEADS * HEAD_DIM
WINDOW = 128
ATT_BLOCK = 128
ROPE_BASE = 10000.0
GRID_W = 64
HG_HEADS = 4
HG_DK = 128
HG_DV = 128
HG_WIDTH = HG_HEADS * HG_DV
GATE_EPS = 1e-6
CV_WIDTH = 512
CONV_TAPS = 31
NEG = -1e30

OFF_AQ = 0
OFF_AK = OFF_AQ + ATT_WIDTH
OFF_AV = OFF_AK + KV_WIDTH
OFF_AG = OFF_AV + KV_WIDTH
OFF_HQ = OFF_AG + ATT_WIDTH
OFF_HI = OFF_HQ + HG_WIDTH
OFF_HFF = OFF_HI + HG_WIDTH
OFF_HFB = OFF_HFF + HG_WIDTH
OFF_HG = OFF_HFB + HG_WIDTH
OFF_CA = OFF_HG + HG_WIDTH
OFF_CB = OFF_CA + CV_WIDTH
OFF_CG = OFF_CB + CV_WIDTH
D_IN = OFF_CG + CV_WIDTH

MOD_ROWS = 8
MOD_TN = 512
IN_TM = 1024
IN_TN = 512
OUT_TM = 512
HG_CHUNK = 256
HG_DIAG = 8
CONV_TR = 256
CONV_HALO = 16
VMEM_LIMIT = 56 * 1024 * 1024


def _cparams(sem):
    return pltpu.CompilerParams(dimension_semantics=sem, vmem_limit_bytes=VMEM_LIMIT)


def _silu(x):
    return x * jax.nn.sigmoid(x)


def _pick_row(ref, r):
    v = ref[...]
    rows = lax.broadcasted_iota(jnp.int32, v.shape, 0)
    return jnp.sum(jnp.where(rows == r, v, 0.0), axis=0, keepdims=True)


def _dot_nt(a, b):
    return lax.dot_general(a, b, (((1,), (1,)), ((), ())), preferred_element_type=F32)


def _mod_kernel(cond_ref, w_ref, b_ref, o_ref):
    s = _silu(cond_ref[...]).astype(BF16)
    o_ref[...] = jnp.dot(s, w_ref[...].astype(BF16), preferred_element_type=F32) + b_ref[...]


def _modulation(conds, ada_w, ada_b):
    depth, d, n = ada_w.shape
    return pl.pallas_call(
        _mod_kernel,
        grid=(depth, n // MOD_TN),
        in_specs=[
            pl.BlockSpec((MOD_ROWS, d), lambda l, j: (0, 0)),
            pl.BlockSpec((None, d, MOD_TN), lambda l, j: (l, 0, j)),
            pl.BlockSpec((None, 1, MOD_TN), lambda l, j: (l, 0, j)),
        ],
        out_specs=pl.BlockSpec((None, MOD_ROWS, MOD_TN), lambda l, j: (l, 0, j)),
        out_shape=jax.ShapeDtypeStruct((depth, MOD_ROWS, n), F32),
        compiler_params=_cparams(("parallel", "parallel")),
        name="modulation",
    )(conds, ada_w, ada_b.reshape(depth, 1, n))


def _in_proj_kernel(x_ref, shift_ref, scale_ref, w_ref, z_ref, xb_ref, *, row0, tiles_per_seq):
    i = pl.program_id(0)

    @pl.when(pl.program_id(1) == 0)
    def _():
        r = row0 + i // tiles_per_seq
        sh = _pick_row(shift_ref, r)
        sc = _pick_row(scale_ref, r)
        xb_ref[...] = (x_ref[...] * (1.0 + sc) + sh).astype(BF16)

    z_ref[...] = jnp.dot(xb_ref[...], w_ref[...], preferred_element_type=F32)


def _in_proj(x2, mod, w_in_bf, layer, row0, seq_len):
    t, d = x2.shape
    tm = min(IN_TM, t)
    tiles_per_seq = max(seq_len // tm, 1) if row0 else t // tm + 1
    kern = functools.partial(_in_proj_kernel, row0=row0, tiles_per_seq=tiles_per_seq)
    return pl.pallas_call(
        kern,
        grid=(t // tm, D_IN // IN_TN),
        in_specs=[
            pl.BlockSpec((tm, d), lambda i, j: (i, 0)),
            pl.BlockSpec((None, MOD_ROWS, d), lambda i, j: (layer, 0, 0)),
            pl.BlockSpec((None, MOD_ROWS, d), lambda i, j: (layer, 0, 1)),
            pl.BlockSpec((None, d, IN_TN), lambda i, j: (layer, 0, j)),
        ],
        out_specs=pl.BlockSpec((tm, IN_TN), lambda i, j: (i, j)),
        out_shape=jax.ShapeDtypeStruct((t, D_IN), F32),
        scratch_shapes=[pltpu.VMEM((tm, d), BF16)],
        compiler_params=_cparams(("parallel", "arbitrary")),
        name="in_proj",
    )(x2, mod, mod, w_in_bf)


def _out_proj_kernel(att_ref, hg_ref, cv_ref, x_ref, gate_ref, w_ref, lnw_ref, lnb_ref, y_ref,
                     *, row0, tiles_per_seq, alpha):
    r = row0 + pl.program_id(0) // tiles_per_seq
    gate = _pick_row(gate_ref, r)
    acc = jnp.dot(att_ref[...], w_ref[0:ATT_WIDTH, :], preferred_element_type=F32)
    acc += jnp.dot(hg_ref[...], w_ref[ATT_WIDTH:ATT_WIDTH + HG_WIDTH, :], preferred_element_type=F32)
    acc += jnp.dot(cv_ref[...], w_ref[ATT_WIDTH + HG_WIDTH:, :], preferred_element_type=F32)
    y = alpha * x_ref[...] + gate * acc
    mu = jnp.mean(y, axis=-1, keepdims=True)
    yc = y - mu
    var = jnp.mean(yc * yc, axis=-1, keepdims=True)
    y_ref[...] = yc * lax.rsqrt(var + 1e-5) * lnw_ref[...] + lnb_ref[...]


def _out_proj(att, hg, cv, x2, mod, w_out_bf, ln_w, ln_b, layer, row0, seq_len, alpha):
    t, d = x2.shape
    tm = min(OUT_TM, t)
    tiles_per_seq = max(seq_len // tm, 1) if row0 else t // tm + 1
    depth = w_out_bf.shape[0]
    kern = functools.partial(_out_proj_kernel, row0=row0, tiles_per_seq=tiles_per_seq, alpha=alpha)
    return pl.pallas_call(
        kern,
        grid=(t // tm,),
        in_specs=[
            pl.BlockSpec((tm, ATT_WIDTH), lambda i: (i, 0)),
            pl.BlockSpec((tm, HG_WIDTH), lambda i: (i, 0)),
            pl.BlockSpec((tm, CV_WIDTH), lambda i: (i, 0)),
            pl.BlockSpec((tm, d), lambda i: (i, 0)),
            pl.BlockSpec((None, MOD_ROWS, d), lambda i: (layer, 0, 2)),
            pl.BlockSpec((None, d, d), lambda i: (layer, 0, 0)),
            pl.BlockSpec((None, 1, d), lambda i: (layer, 0, 0)),
            pl.BlockSpec((None, 1, d), lambda i: (layer, 0, 0)),
        ],
        out_specs=pl.BlockSpec((tm, d), lambda i: (i, 0)),
        out_shape=jax.ShapeDtypeStruct((t, d), F32),
        compiler_params=_cparams(("parallel",)),
        name="out_proj",
    )(att, hg, cv, x2, mod, w_out_bf, ln_w.reshape(depth, 1, d), ln_b.reshape(depth, 1, d))


def _ctx_attn_kernel(sink_ref, q_ref, k_ref, v_ref, g_ref, o_ref, *, layer):
    j = pl.program_id(1)
    scale = HEAD_DIM ** -0.5
    kb = k_ref[...].astype(BF16)
    vb = v_ref[...].astype(BF16)
    for g in range(GQA_GROUP):
        cols = slice(g * HEAD_DIM, (g + 1) * HEAD_DIM)
        sink = sink_ref[layer, j * GQA_GROUP + g]
        s = _dot_nt(q_ref[:, cols].astype(BF16), kb) * scale
        m = jnp.maximum(jnp.max(s, axis=-1, keepdims=True), sink)
        p = jnp.exp(s - m)
        denom = jnp.sum(p, axis=-1, keepdims=True) + jnp.exp(sink - m)
        o = jnp.dot(p.astype(BF16), vb, preferred_element_type=F32) / denom
        o_ref[:, cols] = (o * _silu(g_ref[:, cols])).astype(o_ref.dtype)


def _ctx_attention(z, sink, layer, batch, seq_len):
    t = z.shape[0]
    qw = GQA_GROUP * HEAD_DIM
    kern = functools.partial(_ctx_attn_kernel, layer=layer)
    return pl.pallas_call(
        kern,
        grid=(batch, ATT_KV_HEADS),
        in_specs=[
            pl.BlockSpec(memory_space=pltpu.SMEM),
            pl.BlockSpec((seq_len, qw), lambda b, j: (b, OFF_AQ // qw + j)),
            pl.BlockSpec((seq_len, HEAD_DIM), lambda b, j: (b, OFF_AK // HEAD_DIM + j)),
            pl.BlockSpec((seq_len, HEAD_DIM), lambda b, j: (b, OFF_AV // HEAD_DIM + j)),
            pl.BlockSpec((seq_len, qw), lambda b, j: (b, OFF_AG // qw + j)),
        ],
        out_specs=pl.BlockSpec((seq_len, qw), lambda b, j: (b, j)),
        out_shape=jax.ShapeDtypeStruct((t, ATT_WIDTH), BF16),
        compiler_params=_cparams(("parallel", "parallel")),
        name="ctx_attention",
    )(sink, z, z, z, z)


def _rope_tables(seq_len):
    rows = seq_len // GRID_W
    row = jnp.broadcast_to(jnp.arange(rows)[:, None], (rows, GRID_W)).reshape(-1)
    col = jnp.broadcast_to(jnp.arange(GRID_W)[None, :], (rows, GRID_W)).reshape(-1)
    nf = HEAD_DIM // 4
    inv = ROPE_BASE ** (-jnp.arange(nf, dtype=F32) / nf)
    ang_r = row.astype(F32)[:, None] * inv[None, :]
    ang_c = col.astype(F32)[:, None] * inv[None, :]
    cos = jnp.concatenate([jnp.cos(ang_r), jnp.cos(ang_r), jnp.cos(ang_c), jnp.cos(ang_c)], axis=-1)
    sin = jnp.concatenate([-jnp.sin(ang_r), jnp.sin(ang_r), -jnp.sin(ang_c), jnp.sin(ang_c)], axis=-1)
    return cos, sin


def _rope_kernel(x_ref, cos_ref, sin_ref, o_ref):
    cos = cos_ref[...]
    sin = sin_ref[...]
    nf = HEAD_DIM // 4
    lane = lax.broadcasted_iota(jnp.int32, cos.shape, 1)
    first = (lane % (2 * nf)) < nf
    for h in range(x_ref.shape[1] // HEAD_DIM):
        cols = slice(h * HEAD_DIM, (h + 1) * HEAD_DIM)
        x = x_ref[:, cols]
        partner = jnp.where(first, pltpu.roll(x, HEAD_DIM - nf, 1), pltpu.roll(x, nf, 1))
        o_ref[:, cols] = (x * cos + partner * sin).astype(o_ref.dtype)


def _rope(z, cos, sin, seq_len):
    t = z.shape[0]
    tr = 256
    width = ATT_WIDTH + KV_WIDTH
    per_seq = seq_len // tr
    return pl.pallas_call(
        _rope_kernel,
        grid=(t // tr,),
        in_specs=[
            pl.BlockSpec((tr, width), lambda i: (i, 0)),
            pl.BlockSpec((tr, HEAD_DIM), lambda i: (i % per_seq, 0)),
            pl.BlockSpec((tr, HEAD_DIM), lambda i: (i % per_seq, 0)),
        ],
        out_specs=pl.BlockSpec((tr, width), lambda i: (i, 0)),
        out_shape=jax.ShapeDtypeStruct((t, width), BF16),
        compiler_params=_cparams(("parallel",)),
        name="rope",
    )(z, cos, sin)


def _lat_attn_kernel(sink_ref, q_ref, kp_ref, kc_ref, kn_ref, vp_ref, vc_ref, vn_ref, ck_ref, cv_ref,
                     g_ref, o_ref, kcat_ref, vcat_ref, *, layer, n_blocks):
    n = pl.program_id(1)
    j = pl.program_id(2)
    blk = ATT_BLOCK
    past = ck_ref.shape[0]
    scale = HEAD_DIM ** -0.5
    kcat_ref[0:blk, :] = kp_ref[...]
    kcat_ref[blk:2 * blk, :] = kc_ref[...]
    kcat_ref[2 * blk:3 * blk, :] = kn_ref[...]
    kcat_ref[3 * blk:, :] = ck_ref[...].astype(BF16)
    vcat_ref[0:blk, :] = vp_ref[...].astype(BF16)
    vcat_ref[blk:2 * blk, :] = vc_ref[...].astype(BF16)
    vcat_ref[2 * blk:3 * blk, :] = vn_ref[...].astype(BF16)
    vcat_ref[3 * blk:, :] = cv_ref[...].astype(BF16)

    q4 = jnp.concatenate([q_ref[:, g * HEAD_DIM:(g + 1) * HEAD_DIM] for g in range(GQA_GROUP)], axis=0)
    s = _dot_nt(q4, kcat_ref[...]) * scale
    nkeys = 3 * blk + past
    qi = lax.broadcasted_iota(jnp.int32, (blk, nkeys), 0)
    col = lax.broadcasted_iota(jnp.int32, (blk, nkeys), 1)
    prev_lo = jnp.where(n > 0, 0, blk)
    next_hi = jnp.where(n < n_blocks - 1, 3 * blk, 2 * blk)
    ok_prev = (col < blk) & (col >= qi) & (col >= prev_lo)
    ok_next = (col >= 2 * blk) & (col < next_hi) & (col - 2 * blk <= qi)
    ok = ok_prev | ((col >= blk) & (col < 2 * blk)) | ok_next | (col >= 3 * blk)
    bias = jnp.where(ok, 0.0, NEG)
    s = s + jnp.concatenate([bias] * GQA_GROUP, axis=0)

    rsel = lax.broadcasted_iota(jnp.int32, (GQA_GROUP * blk, 1), 0) // blk
    sink = jnp.zeros((GQA_GROUP * blk, 1), F32)
    for g in range(GQA_GROUP):
        sink = jnp.where(rsel == g, sink_ref[layer, j * GQA_GROUP + g], sink)
    m = jnp.maximum(jnp.max(s, axis=-1, keepdims=True), sink)
    p = jnp.exp(s - m)
    denom = jnp.sum(p, axis=-1, keepdims=True) + jnp.exp(sink - m)
    o = jnp.dot(p.astype(BF16), vcat_ref[...], preferred_element_type=F32) / denom
    for g in range(GQA_GROUP):
        cols = slice(g * HEAD_DIM, (g + 1) * HEAD_DIM)
        o_ref[:, cols] = (o[g * blk:(g + 1) * blk, :] * _silu(g_ref[:, cols])).astype(o_ref.dtype)


def _lat_attention(z, qk, ck, cv, sink, layer, batch, seq_len):
    t = z.shape[0]
    blk = ATT_BLOCK
    nb = seq_len // blk
    qw = GQA_GROUP * HEAD_DIM
    past = ck.shape[2]
    kcol = OFF_AK // HEAD_DIM
    vcol = OFF_AV // HEAD_DIM

    def prev(b, n, j):
        return b * nb + jnp.maximum(n - 1, 0)

    def cur(b, n, j):
        return b * nb + n

    def nxt(b, n, j):
        return b * nb + jnp.minimum(n + 1, nb - 1)

    kern = functools.partial(_lat_attn_kernel, layer=layer, n_blocks=nb)
    return pl.pallas_call(
        kern,
        grid=(batch, nb, ATT_KV_HEADS),
        in_specs=[
            pl.BlockSpec(memory_space=pltpu.SMEM),
            pl.BlockSpec((blk, qw), lambda b, n, j: (cur(b, n, j), j)),
            pl.BlockSpec((blk, HEAD_DIM), lambda b, n, j: (prev(b, n, j), kcol + j)),
            pl.BlockSpec((blk, HEAD_DIM), lambda b, n, j: (cur(b, n, j), kcol + j)),
            pl.BlockSpec((blk, HEAD_DIM), lambda b, n, j: (nxt(b, n, j), kcol + j)),
            pl.BlockSpec((blk, HEAD_DIM), lambda b, n, j: (prev(b, n, j), vcol + j)),
            pl.BlockSpec((blk, HEAD_DIM), lambda b, n, j: (cur(b, n, j), vcol + j)),
            pl.BlockSpec((blk, HEAD_DIM), lambda b, n, j: (nxt(b, n, j), vcol + j)),
            pl.BlockSpec((None, None, past, HEAD_DIM), lambda b, n, j: (b, layer, 0, j)),
            pl.BlockSpec((None, None, past, HEAD_DIM), lambda b, n, j: (b, layer, 0, j)),
            pl.BlockSpec((blk, qw), lambda b, n, j: (cur(b, n, j), OFF_AG // qw + j)),
        ],
        out_specs=pl.BlockSpec((blk, qw), lambda b, n, j: (cur(b, n, j), j)),
        out_shape=jax.ShapeDtypeStruct((t, ATT_WIDTH), BF16),
        scratch_shapes=[pltpu.VMEM((3 * blk + past, HEAD_DIM), BF16),
                        pltpu.VMEM((3 * blk + past, HEAD_DIM), BF16)],
        compiler_params=_cparams(("parallel", "parallel", "parallel")),
        name="lat_attention",
    )(sink, qk, qk, qk, qk, z, z, z, ck, cv, z)


def _hgrn_level_table(c):
    t = np.arange(c)[:, None]
    s = np.arange(c)[None, :]
    x = (t // HG_DIAG) ^ (s // HG_DIAG)
    lvl = np.where(x == 0, 0, np.floor(np.log2(np.maximum(x, 1))).astype(np.int64) + 1)
    return np.where(s <= t, lvl, -1).astype(np.int32)


def _block_rows(g, blk, row_in_blk):
    c, w = g.shape
    pieces = [jnp.broadcast_to(g[p * blk + row_in_blk:p * blk + row_in_blk + 1, :], (blk, w))
              for p in range(c // blk)]
    return pieces[0] if len(pieces) == 1 else jnp.concatenate(pieces, axis=0)


def _hgrn_chunk_dir(q, kk, logf, vb, vtb, st_prev, tri, lvl, reverse):
    c = q.shape[0]
    h1 = logf.astype(BF16)
    r1 = logf - h1.astype(F32)
    h2 = r1.astype(BF16)
    h3 = (r1 - h2.astype(F32)).astype(BF16)
    gcum = (jnp.dot(tri, h1, preferred_element_type=F32) + jnp.dot(tri, h2, preferred_element_type=F32)
            + jnp.dot(tri, h3, preferred_element_type=F32))

    mid = HG_DIAG // 2 if reverse else HG_DIAG // 2 - 1
    d = gcum - _block_rows(gcum, HG_DIAG, mid)
    a = _dot_nt((q * jnp.exp(d)).astype(BF16), (kk * jnp.exp(-d)).astype(BF16))
    a = jnp.where(lvl == 0, a, 0.0)
    half = HG_DIAG
    k = 1
    while half < c:
        ref_row = half if reverse else half - 1
        e = jnp.exp(-jnp.abs(gcum - _block_rows(gcum, 2 * half, ref_row)))
        a_k = _dot_nt((q * e).astype(BF16), (kk * e).astype(BF16))
        a = jnp.where(lvl == k, a_k, a)
        half *= 2
        k += 1

    o = jnp.dot(a.astype(BF16), vb, preferred_element_type=F32)
    o += _dot_nt((q * jnp.exp(gcum)).astype(BF16), st_prev.astype(BF16))
    last = gcum[0:1, :] if reverse else gcum[c - 1:c, :]
    kd = (kk * jnp.exp(last - gcum)).astype(BF16)
    st_new = st_prev * jnp.exp(last) + jnp.dot(vtb, kd, preferred_element_type=F32)
    return o, st_new


def _chunk_rows(ci, chunk):
    if isinstance(ci, int):
        return pl.ds(ci * chunk, chunk)
    return pl.ds(pl.multiple_of(ci * chunk, chunk), chunk)


def _hgrn_kernel(*refs, layer, n_chunks, chunk, has_s0, want_state):
    hq_ref, hi_ref, hff_ref, hfb_ref, hgt_ref, lb_ref, nw_ref, tri_ref, lvl_ref = refs[:9]
    pos = 9
    s0_ref = None
    if has_s0:
        s0_ref = refs[pos]
        pos += 1
    out_ref = refs[pos]
    pos += 1
    st_ref = None
    if want_state:
        st_ref = refs[pos]
        pos += 1
    of_ref, ob_ref, sf_ref, sb_ref = refs[pos:pos + 4]

    lbs = []
    for dirn in range(2):
        raw = lb_ref[dirn]
        ex = jnp.exp(raw - jnp.max(raw, axis=0, keepdims=True))
        sm = ex / jnp.sum(ex, axis=0, keepdims=True)
        lb = jnp.zeros((1, HG_DK), F32)
        for l in range(1, layer + 1):
            lb = lb + sm[l:l + 1, :]
        lbs.append(lb)

    if has_s0:
        sf_ref[...] = s0_ref[0].T
        sb_ref[...] = s0_ref[1].T
    else:
        sf_ref[...] = jnp.zeros_like(sf_ref)
        sb_ref[...] = jnp.zeros_like(sb_ref)

    def gates(hf, lb):
        one_minus_f = (1.0 - lb) * jax.nn.sigmoid(-hf)
        return one_minus_f, jnp.log1p(-jnp.minimum(one_minus_f, 1.0 - GATE_EPS))

    def load(ci):
        rows = _chunk_rows(ci, chunk)
        q = _silu(hq_ref[rows, :]) * (HG_DK ** -0.5)
        v = hi_ref[rows, :]
        return rows, q, v.astype(BF16), v.T.astype(BF16)

    def step(i, carry):
        rows, q, vb, vtb = load(i)
        kf, gf = gates(hff_ref[rows, :], lbs[0])
        o, st = _hgrn_chunk_dir(q, kf, gf, vb, vtb, sf_ref[...], tri_ref[0], lvl_ref[0], False)
        of_ref[rows, :] = o
        sf_ref[...] = st
        rows, q, vb, vtb = load(n_chunks - 1 - i)
        kb, gb = gates(hfb_ref[rows, :], lbs[1])
        o, st = _hgrn_chunk_dir(q, kb, gb, vb, vtb, sb_ref[...], tri_ref[1], lvl_ref[1], True)
        ob_ref[rows, :] = o
        sb_ref[...] = st
        return carry

    if n_chunks == 1:
        step(0, 0)
    else:
        lax.fori_loop(0, n_chunks, step, 0)

    def finish(i, carry):
        rows = _chunk_rows(i, chunk)
        o = of_ref[rows, :] + ob_ref[rows, :]
        o = o * lax.rsqrt(jnp.mean(o * o, axis=-1, keepdims=True) + 1e-6) * nw_ref[...]
        out_ref[rows, :] = (o * _silu(hgt_ref[rows, :])).astype(out_ref.dtype)
        return carry

    if n_chunks == 1:
        finish(0, 0)
    else:
        lax.fori_loop(0, n_chunks, finish, 0)

    if want_state:
        st_ref[0] = sf_ref[...].T
        st_ref[1] = sb_ref[...].T


def _hgrn(z, lb_raw, norm_w, s0, layer, batch, seq_len, want_state):
    t = z.shape[0]
    chunk = min(HG_CHUNK, seq_len)
    n_chunks = seq_len // chunk
    depth = norm_w.shape[0]
    has_s0 = s0 is not None
    lvl_f = _hgrn_level_table(chunk)
    lvl = jnp.asarray(np.stack([lvl_f, lvl_f.T]))
    low = np.tril(np.ones((chunk, chunk), np.float32))
    tri = jnp.asarray(np.stack([low, low.T]), dtype=BF16)

    def zcol(off):
        return pl.BlockSpec((seq_len, HG_DK), lambda b, h: (b, off // HG_DK + h))

    in_specs = [zcol(OFF_HQ), zcol(OFF_HI), zcol(OFF_HFF), zcol(OFF_HFB), zcol(OFF_HG),
                pl.BlockSpec((2, depth, HG_DK), lambda b, h: (0, 0, h)),
                pl.BlockSpec((None, 1, HG_DV), lambda b, h: (layer, 0, h)),
                pl.BlockSpec((2, chunk, chunk), lambda b, h: (0, 0, 0)),
                pl.BlockSpec((2, chunk, chunk), lambda b, h: (0, 0, 0))]
    args = [z, z, z, z, z, lb_raw, norm_w.reshape(depth, 1, HG_WIDTH), tri, lvl]
    if has_s0:
        in_specs.append(pl.BlockSpec((None, None, 2, None, HG_DK, HG_DV), lambda b, h: (b, layer, 0, h, 0, 0)))
        args.append(s0)
    out_specs = [pl.BlockSpec((seq_len, HG_DV), lambda b, h: (b, h))]
    out_shape = [jax.ShapeDtypeStruct((t, HG_WIDTH), BF16)]
    if want_state:
        out_specs.append(pl.BlockSpec((None, 2, None, HG_DK, HG_DV), lambda b, h: (b, 0, h, 0, 0)))
        out_shape.append(jax.ShapeDtypeStruct((batch, 2, HG_HEADS, HG_DK, HG_DV), F32))
    kern = functools.partial(_hgrn_kernel, layer=layer, n_chunks=n_chunks, chunk=chunk,
                             has_s0=has_s0, want_state=want_state)
    res = pl.pallas_call(
        kern,
        grid=(batch, HG_HEADS),
        in_specs=in_specs,
        out_specs=out_specs,
        out_shape=out_shape,
        scratch_shapes=[pltpu.VMEM((seq_len, HG_DV), F32), pltpu.VMEM((seq_len, HG_DV), F32),
                        pltpu.VMEM((HG_DV, HG_DK), F32), pltpu.VMEM((HG_DV, HG_DK), F32)],
        compiler_params=_cparams(("parallel", "parallel")),
        name="hgrn",
    )(*args)
    return (res[0], res[1]) if want_state else (res[0], None)


def _conv_kernel(ap_ref, ac_ref, an_ref, bp_ref, bc_ref, bn_ref, g_ref, w_ref, b_ref, lnw_ref, lnb_ref,
                 o_ref, u_ref, y_ref, *, tiles_per_seq):
    i = pl.program_id(0) % tiles_per_seq
    tr = ac_ref.shape[0]
    halo = ap_ref.shape[0]
    keep_prev = jnp.where(i > 0, 1.0, 0.0)
    keep_next = jnp.where(i < tiles_per_seq - 1, 1.0, 0.0)
    u_ref[0:halo, :] = ap_ref[...] * jax.nn.sigmoid(bp_ref[...]) * keep_prev
    u_ref[halo:halo + tr, :] = ac_ref[...] * jax.nn.sigmoid(bc_ref[...])
    u_ref[halo + tr:, :] = an_ref[...] * jax.nn.sigmoid(bn_ref[...]) * keep_next

    lanes = 128
    base = halo - CONV_TAPS // 2
    for cgrp in range(CV_WIDTH // lanes):
        cols = slice(cgrp * lanes, (cgrp + 1) * lanes)
        acc = jnp.zeros((tr, lanes), F32)
        for tap in range(CONV_TAPS):
            acc = acc + u_ref[base + tap:base + tap + tr, cols] * w_ref[tap:tap + 1, cols]
        y_ref[:, cols] = acc + b_ref[:, cols]

    y = y_ref[...]
    mu = jnp.mean(y, axis=-1, keepdims=True)
    yc = y - mu
    var = jnp.mean(yc * yc, axis=-1, keepdims=True)
    yn = yc * lax.rsqrt(var + 1e-5) * lnw_ref[...] + lnb_ref[...]
    o_ref[...] = (_silu(yn) * _silu(g_ref[...])).astype(o_ref.dtype)


def _conv(z, conv_w, conv_b, ln_w, ln_b, layer, seq_len):
    t = z.shape[0]
    depth = conv_w.shape[0]
    tr = min(CONV_TR, seq_len)
    halo = CONV_HALO
    tiles_per_seq = seq_len // tr
    hb = tr // halo
    n_halo = t // halo
    ca = OFF_CA // CV_WIDTH
    cb = OFF_CB // CV_WIDTH
    cg = OFF_CG // CV_WIDTH

    def prev(i):
        return jnp.maximum(i * hb - 1, 0)

    def nxt(i):
        return jnp.minimum((i + 1) * hb, n_halo - 1)

    kern = functools.partial(_conv_kernel, tiles_per_seq=tiles_per_seq)
    vec = lambda a: a.reshape(depth, 1, CV_WIDTH)
    vspec = pl.BlockSpec((None, 1, CV_WIDTH), lambda i: (layer, 0, 0))
    return pl.pallas_call(
        kern,
        grid=(t // tr,),
        in_specs=[
            pl.BlockSpec((halo, CV_WIDTH), lambda i: (prev(i), ca)),
            pl.BlockSpec((tr, CV_WIDTH), lambda i: (i, ca)),
            pl.BlockSpec((halo, CV_WIDTH), lambda i: (nxt(i), ca)),
            pl.BlockSpec((halo, CV_WIDTH), lambda i: (prev(i), cb)),
            pl.BlockSpec((tr, CV_WIDTH), lambda i: (i, cb)),
            pl.BlockSpec((halo, CV_WIDTH), lambda i: (nxt(i), cb)),
            pl.BlockSpec((tr, CV_WIDTH), lambda i: (i, cg)),
            pl.BlockSpec((None, CONV_TAPS, CV_WIDTH), lambda i: (layer, 0, 0)),
            vspec, vspec, vspec,
        ],
        out_specs=pl.BlockSpec((tr, CV_WIDTH), lambda i: (i, 0)),
        out_shape=jax.ShapeDtypeStruct((t, CV_WIDTH), BF16),
        scratch_shapes=[pltpu.VMEM((tr + 2 * halo, CV_WIDTH), F32), pltpu.VMEM((tr, CV_WIDTH), F32)],
        compiler_params=_cparams(("parallel",)),
        name="conv",
    )(z, z, z, z, z, z, z, conv_w, vec(conv_b), vec(ln_w), vec(ln_b))


def kernel(x_prompt, x_sample, cache_k, cache_v, state_hgrn, c, c_ctx, ada_w, ada_b, w_in, w_out,
           attn_sink, hg_lower_bounds, hg_norm_w, conv_w, conv_b, conv_ln_w, conv_ln_b, ln_w, ln_b):
    batch, seq, d = x_prompt.shape
    dec_batch, dec_seq, _ = x_sample.shape
    depth = w_in.shape[0]
    past = cache_k.shape[2]
    alpha = float((2 * depth) ** 0.25)
    assert dec_batch + 1 <= MOD_ROWS and w_in.shape[2] == D_IN

    conds = jnp.concatenate([c_ctx[None, :], c, jnp.zeros((MOD_ROWS - 1 - dec_batch, d), F32)], axis=0)
    mod = _modulation(conds, ada_w, ada_b)
    w_in_bf = w_in.astype(BF16)
    w_out_bf = w_out.astype(BF16)
    cos, sin = _rope_tables(dec_seq)
    ck = cache_k.reshape(dec_batch, depth, past, KV_WIDTH)
    cv = cache_v.reshape(dec_batch, depth, past, KV_WIDTH)

    y = x_prompt.reshape(batch * seq, d)
    ks, vs, ss = [], [], []
    for l in range(depth):
        z = _in_proj(y, mod, w_in_bf, l, 0, seq)
        att = _ctx_attention(z, attn_sink, l, batch, seq)
        hg, st = _hgrn(z, hg_lower_bounds, hg_norm_w, None, l, batch, seq, True)
        cvo = _conv(z, conv_w, conv_b, conv_ln_w, conv_ln_b, l, seq)
        y = _out_proj(att, hg, cvo, y, mod, w_out_bf, ln_w, ln_b, l, 0, seq, alpha)
        ks.append(z[:, OFF_AK:OFF_AK + KV_WIDTH].reshape(batch, seq, ATT_KV_HEADS, HEAD_DIM))
        vs.append(z[:, OFF_AV:OFF_AV + KV_WIDTH].reshape(batch, seq, ATT_KV_HEADS, HEAD_DIM))
        ss.append(st)
    y_prompt = y.reshape(batch, seq, d)
    new_cache_k = jnp.stack(ks, axis=1)
    new_cache_v = jnp.stack(vs, axis=1)
    new_state = jnp.stack(ss, axis=1)

    y = x_sample.reshape(dec_batch * dec_seq, d)
    for l in range(depth):
        z = _in_proj(y, mod, w_in_bf, l, 1, dec_seq)
        qk = _rope(z, cos, sin, dec_seq)
        att = _lat_attention(z, qk, ck, cv, attn_sink, l, dec_batch, dec_seq)
        hg, _ = _hgrn(z, hg_lower_bounds, hg_norm_w, state_hgrn, l, dec_batch, dec_seq, False)
        cvo = _conv(z, conv_w, conv_b, conv_ln_w, conv_ln_b, l, dec_seq)
        y = _out_proj(att, hg, cvo, y, mod, w_out_bf, ln_w, ln_b, l, 1, dec_seq, alpha)
    y_sample = y.reshape(dec_batch, dec_seq, d)

    return (y_prompt, y_sample, new_cache_k, new_cache_v, new_state)
```

```python
import functools

import numpy as np
import jax
import jax.numpy as jnp
from jax import lax
from jax.experimental import pallas as pl
from jax.experimental.pallas import tpu as pltpu

F32 = jnp.float32
BF16 = jnp.bfloat16

ATT_HEADS = 8
ATT_KV_HEADS = 2
GQA_GROUP = ATT_HEADS // ATT_KV_HEADS
HEAD_DIM = 128
ATT_WIDTH = ATT_HEADS * HEAD_DIM
KV_WIDTH = ATT_KV_HEADS * HEAD_DIM
WINDOW = 128
ATT_BLOCK = 128
ROPE_BASE = 10000.0
GRID_W = 64
HG_HEADS = 4
HG_DK = 128
HG_DV = 128
HG_WIDTH = HG_HEADS * HG_DV
GATE_EPS = 1e-6
CV_WIDTH = 512
CONV_TAPS = 31
NEG = -1e30

OFF_AQ = 0
OFF_AK = OFF_AQ + ATT_WIDTH
OFF_AV = OFF_AK + KV_WIDTH
OFF_AG = OFF_AV + KV_WIDTH
OFF_HQ = OFF_AG + ATT_WIDTH
OFF_HI = OFF_HQ + HG_WIDTH
OFF_HFF = OFF_HI + HG_WIDTH
OFF_HFB = OFF_HFF + HG_WIDTH
OFF_HG = OFF_HFB + HG_WIDTH
OFF_CA = OFF_HG + HG_WIDTH
OFF_CB = OFF_CA + CV_WIDTH
OFF_CG = OFF_CB + CV_WIDTH
D_IN = OFF_CG + CV_WIDTH

MOD_ROWS = 8
MOD_TN = 512
IN_TM = 1024
IN_TN = 1664
OUT_TM = 512
HG_CHUNK = 256
HG_DIAG = 8
CONV_TR = 256
CONV_HALO = 16
VMEM_LIMIT = 56 * 1024 * 1024


def _cparams(sem):
    return pltpu.CompilerParams(dimension_semantics=sem, vmem_limit_bytes=VMEM_LIMIT)


def _silu(x):
    return x * jax.nn.sigmoid(x)


def _pick_row(ref, r):
    v = ref[...]
    rows = lax.broadcasted_iota(jnp.int32, v.shape, 0)
    return jnp.sum(jnp.where(rows == r, v, 0.0), axis=0, keepdims=True)


def _dot_nt(a, b):
    return lax.dot_general(a, b, (((1,), (1,)), ((), ())), preferred_element_type=F32)


def _mod_kernel(cond_ref, w_ref, b_ref, o_ref):
    s = _silu(cond_ref[...]).astype(BF16)
    o_ref[...] = jnp.dot(s, w_ref[...].astype(BF16), preferred_element_type=F32) + b_ref[...]


def _modulation(conds, ada_w, ada_b):
    depth, d, n = ada_w.shape
    return pl.pallas_call(
        _mod_kernel,
        grid=(depth, n // MOD_TN),
        in_specs=[
            pl.BlockSpec((MOD_ROWS, d), lambda l, j: (0, 0)),
            pl.BlockSpec((None, d, MOD_TN), lambda l, j: (l, 0, j)),
            pl.BlockSpec((None, 1, MOD_TN), lambda l, j: (l, 0, j)),
        ],
        out_specs=pl.BlockSpec((None, MOD_ROWS, MOD_TN), lambda l, j: (l, 0, j)),
        out_shape=jax.ShapeDtypeStruct((depth, MOD_ROWS, n), F32),
        compiler_params=_cparams(("parallel", "parallel")),
        name="modulation",
    )(conds, ada_w, ada_b.reshape(depth, 1, n))


def _in_proj_kernel(*refs, row0, tiles_per_seq, kv_tile, kv_col, n_cache_in):
    x_ref, shift_ref, scale_ref, w_ref = refs[:4]
    outs = refs[4 + n_cache_in:]
    z_ref, xb_ref = outs[0], outs[-1]
    i = pl.program_id(0)
    j = pl.program_id(1)

    @pl.when(j == 0)
    def _():
        r = row0 + i // tiles_per_seq
        sh = _pick_row(shift_ref, r)
        sc = _pick_row(scale_ref, r)
        xb_ref[...] = (x_ref[...] * (1.0 + sc) + sh).astype(BF16)

    z_ref[...] = jnp.dot(xb_ref[...], w_ref[...], preferred_element_type=F32)

    if kv_tile is not None:
        kc_ref, vc_ref = outs[1], outs[2]

        @pl.when(j == kv_tile)
        def _():
            kc_ref[...] = z_ref[:, kv_col:kv_col + KV_WIDTH].reshape(kc_ref.shape)
            vc_ref[...] = z_ref[:, kv_col + KV_WIDTH:kv_col + 2 * KV_WIDTH].reshape(vc_ref.shape)


def _in_proj(x2, mod, w_in_bf, layer, row0, seq_len, caches=None, want_cache=False):
    t, d = x2.shape
    depth = w_in_bf.shape[0]
    tm = min(IN_TM, t)
    tn = IN_TN
    tiles_per_seq = max(seq_len // tm, 1) if row0 else t // tm + 1
    in_specs = [
        pl.BlockSpec((tm, d), lambda i, j: (i, 0)),
        pl.BlockSpec((None, MOD_ROWS, d), lambda i, j: (layer, 0, 0)),
        pl.BlockSpec((None, MOD_ROWS, d), lambda i, j: (layer, 0, 1)),
        pl.BlockSpec((None, d, tn), lambda i, j: (layer, 0, j)),
    ]
    args = [x2, mod, mod, w_in_bf]
    out_specs = [pl.BlockSpec((tm, tn), lambda i, j: (i, j))]
    out_shape = [jax.ShapeDtypeStruct((t, D_IN), F32)]
    aliases = {}
    kv_tile = kv_col = None
    n_cache_in = 0
    if want_cache:
        kv_tile = OFF_AK // tn
        kv_col = OFF_AK - kv_tile * tn
        assert kv_col + 2 * KV_WIDTH <= tn and tm % seq_len == 0 and OFF_AV == OFF_AK + KV_WIDTH
        cshape = (t // seq_len, depth, seq_len, KV_WIDTH)
        cspec = pl.BlockSpec((tm // seq_len, None, seq_len, KV_WIDTH), lambda i, j: (i, layer, 0, 0))
        out_specs += [cspec, cspec]
        out_shape += [jax.ShapeDtypeStruct(cshape, F32)] * 2
        if caches is not None:
            n_cache_in = 2
            in_specs += [pl.BlockSpec(memory_space=pl.ANY)] * 2
            args += list(caches)
            aliases = {4: 1, 5: 2}
    kern = functools.partial(_in_proj_kernel, row0=row0, tiles_per_seq=tiles_per_seq,
                             kv_tile=kv_tile, kv_col=kv_col, n_cache_in=n_cache_in)
    res = pl.pallas_call(
        kern,
        grid=(t // tm, D_IN // tn),
        in_specs=in_specs,
        out_specs=out_specs,
        out_shape=out_shape,
        input_output_aliases=aliases,
        scratch_shapes=[pltpu.VMEM((tm, d), BF16)],
        compiler_params=_cparams(("parallel", "arbitrary")),
        name="in_proj",
    )(*args)
    return (res[0], (res[1], res[2])) if want_cache else (res[0], None)


def _out_proj_kernel(mix_ref, x_ref, gate_ref, w_ref, lnw_ref, lnb_ref, y_ref,
                     *, row0, tiles_per_seq, alpha):
    r = row0 + pl.program_id(0) // tiles_per_seq
    gate = _pick_row(gate_ref, r)
    acc = jnp.dot(mix_ref[...], w_ref[...], preferred_element_type=F32)
    y = alpha * x_ref[...] + gate * acc
    mu = jnp.mean(y, axis=-1, keepdims=True)
    yc = y - mu
    var = jnp.mean(yc * yc, axis=-1, keepdims=True)
    y_ref[...] = yc * lax.rsqrt(var + 1e-5) * lnw_ref[...] + lnb_ref[...]


def _out_proj(mix, x2, mod, w_out_bf, ln_w, ln_b, layer, row0, seq_len, alpha):
    t, d = x2.shape
    tm = min(OUT_TM, t)
    tiles_per_seq = max(seq_len // tm, 1) if row0 else t // tm + 1
    depth = w_out_bf.shape[0]
    kern = functools.partial(_out_proj_kernel, row0=row0, tiles_per_seq=tiles_per_seq, alpha=alpha)
    return pl.pallas_call(
        kern,
        grid=(t // tm,),
        in_specs=[
            pl.BlockSpec((tm, mix.shape[1]), lambda i: (i, 0)),
            pl.BlockSpec((tm, d), lambda i: (i, 0)),
            pl.BlockSpec((None, MOD_ROWS, d), lambda i: (layer, 0, 2)),
            pl.BlockSpec((None, d, d), lambda i: (layer, 0, 0)),
            pl.BlockSpec((None, 1, d), lambda i: (layer, 0, 0)),
            pl.BlockSpec((None, 1, d), lambda i: (layer, 0, 0)),
        ],
        out_specs=pl.BlockSpec((tm, d), lambda i: (i, 0)),
        out_shape=jax.ShapeDtypeStruct((t, d), F32),
        compiler_params=_cparams(("parallel",)),
        name="out_proj",
    )(mix, x2, mod, w_out_bf, ln_w.reshape(depth, 1, d), ln_b.reshape(depth, 1, d))


def _gate_cols(g_refs, h):
    per = g_refs[0].shape[1] // HEAD_DIM
    return g_refs[h // per][:, (h % per) * HEAD_DIM:(h % per + 1) * HEAD_DIM]


def _ctx_attn_kernel(sink_ref, q_ref, kv_ref, g0_ref, g1_ref, o_ref, *, layer):
    scale = HEAD_DIM ** -0.5
    for j in range(ATT_KV_HEADS):
        kb = kv_ref[:, j * HEAD_DIM:(j + 1) * HEAD_DIM].astype(BF16)
        vb = kv_ref[:, KV_WIDTH + j * HEAD_DIM:KV_WIDTH + (j + 1) * HEAD_DIM].astype(BF16)
        for g in range(GQA_GROUP):
            h = j * GQA_GROUP + g
            cols = slice(h * HEAD_DIM, (h + 1) * HEAD_DIM)
            sink = sink_ref[layer, h]
            s = _dot_nt(q_ref[:, cols].astype(BF16), kb) * scale
            m = jnp.maximum(jnp.max(s, axis=-1, keepdims=True), sink)
            p = jnp.exp(s - m)
            denom = jnp.sum(p, axis=-1, keepdims=True) + jnp.exp(sink - m)
            o = jnp.dot(p.astype(BF16), vb, preferred_element_type=F32) / denom
            o_ref[:, cols] = (o * _silu(_gate_cols((g0_ref, g1_ref), h))).astype(o_ref.dtype)


def _ctx_attention(z, sink, layer, batch, seq_len, d_mix):
    t = z.shape[0]
    gw = ATT_WIDTH // 2
    kern = functools.partial(_ctx_attn_kernel, layer=layer)
    return pl.pallas_call(
        kern,
        grid=(batch,),
        in_specs=[
            pl.BlockSpec(memory_space=pltpu.SMEM),
            pl.BlockSpec((seq_len, ATT_WIDTH), lambda b: (b, OFF_AQ // ATT_WIDTH)),
            pl.BlockSpec((seq_len, 2 * KV_WIDTH), lambda b: (b, OFF_AK // (2 * KV_WIDTH))),
            pl.BlockSpec((seq_len, gw), lambda b: (b, OFF_AG // gw)),
            pl.BlockSpec((seq_len, gw), lambda b: (b, OFF_AG // gw + 1)),
        ],
        out_specs=pl.BlockSpec((seq_len, ATT_WIDTH), lambda b: (b, 0)),
        out_shape=jax.ShapeDtypeStruct((t, d_mix), BF16),
        compiler_params=_cparams(("parallel",)),
        name="ctx_attention",
    )(sink, z, z, z, z)


def _rope_tables(seq_len):
    pos = np.arange(seq_len)
    nf = HEAD_DIM // 4
    inv = ROPE_BASE ** (-np.arange(nf, dtype=np.float64) / nf)
    ang_r = (pos // GRID_W)[:, None] * inv[None, :]
    ang_c = (pos % GRID_W)[:, None] * inv[None, :]
    cos = np.concatenate([np.cos(ang_r), np.cos(ang_r), np.cos(ang_c), np.cos(ang_c)], axis=-1)
    sin = np.concatenate([-np.sin(ang_r), np.sin(ang_r), -np.sin(ang_c), np.sin(ang_c)], axis=-1)
    return jnp.asarray(cos, F32), jnp.asarray(sin, F32)


def _rope_kernel(x_ref, cos_ref, sin_ref, o_ref):
    cos = cos_ref[...]
    sin = sin_ref[...]
    nf = HEAD_DIM // 4
    lane = lax.broadcasted_iota(jnp.int32, cos.shape, 1)
    first = (lane % (2 * nf)) < nf
    for h in range(x_ref.shape[1] // HEAD_DIM):
        cols = slice(h * HEAD_DIM, (h + 1) * HEAD_DIM)
        x = x_ref[:, cols]
        partner = jnp.where(first, pltpu.roll(x, HEAD_DIM - nf, 1), pltpu.roll(x, nf, 1))
        o_ref[:, cols] = (x * cos + partner * sin).astype(o_ref.dtype)


def _rope(z, cos, sin, seq_len):
    t = z.shape[0]
    tr = 256
    width = ATT_WIDTH + KV_WIDTH
    per_seq = seq_len // tr
    return pl.pallas_call(
        _rope_kernel,
        grid=(t // tr,),
        in_specs=[
            pl.BlockSpec((tr, width), lambda i: (i, 0)),
            pl.BlockSpec((tr, HEAD_DIM), lambda i: (i % per_seq, 0)),
            pl.BlockSpec((tr, HEAD_DIM), lambda i: (i % per_seq, 0)),
        ],
        out_specs=pl.BlockSpec((tr, width), lambda i: (i, 0)),
        out_shape=jax.ShapeDtypeStruct((t, width), BF16),
        compiler_params=_cparams(("parallel",)),
        name="rope",
    )(z, cos, sin)


def _lat_attn_kernel(sink_ref, q_ref, kp_ref, kc_ref, kn_ref, vp_ref, vc_ref, vn_ref, ck_ref, cv_ref,
                     g0_ref, g1_ref, o_ref, *, layer, n_blocks):
    n = pl.program_id(1)
    blk = ATT_BLOCK
    past = ck_ref.shape[0]
    scale = HEAD_DIM ** -0.5
    nkeys = 3 * blk + past
    qi = lax.broadcasted_iota(jnp.int32, (blk, nkeys), 0)
    col = lax.broadcasted_iota(jnp.int32, (blk, nkeys), 1)
    prev_lo = jnp.where(n > 0, 0, blk)
    next_hi = jnp.where(n < n_blocks - 1, 3 * blk, 2 * blk)
    ok_prev = (col < blk) & (col >= qi) & (col >= prev_lo)
    ok_next = (col >= 2 * blk) & (col < next_hi) & (col - 2 * blk <= qi)
    ok = ok_prev | ((col >= blk) & (col < 2 * blk)) | ok_next | (col >= 3 * blk)
    bias = jnp.where(ok, 0.0, NEG)
    bias = jnp.concatenate([bias] * GQA_GROUP, axis=0)
    rsel = lax.broadcasted_iota(jnp.int32, (GQA_GROUP * blk, 1), 0) // blk

    for j in range(ATT_KV_HEADS):
        hcols = slice(j * HEAD_DIM, (j + 1) * HEAD_DIM)
        kcat = jnp.concatenate([kp_ref[:, hcols], kc_ref[:, hcols], kn_ref[:, hcols],
                                ck_ref[:, hcols].astype(BF16)], axis=0)
        vcat = jnp.concatenate([vp_ref[:, hcols].astype(BF16), vc_ref[:, hcols].astype(BF16),
                                vn_ref[:, hcols].astype(BF16), cv_ref[:, hcols].astype(BF16)], axis=0)
        q4 = jnp.concatenate([q_ref[:, (j * GQA_GROUP + g) * HEAD_DIM:(j * GQA_GROUP + g + 1) * HEAD_DIM]
                              for g in range(GQA_GROUP)], axis=0)
        s = _dot_nt(q4, kcat) * scale + bias
        sink = jnp.zeros((GQA_GROUP * blk, 1), F32)
        for g in range(GQA_GROUP):
            sink = jnp.where(rsel == g, sink_ref[layer, j * GQA_GROUP + g], sink)
        m = jnp.maximum(jnp.max(s, axis=-1, keepdims=True), sink)
        p = jnp.exp(s - m)
        denom = jnp.sum(p, axis=-1, keepdims=True) + jnp.exp(sink - m)
        o = jnp.dot(p.astype(BF16), vcat, preferred_element_type=F32) / denom
        for g in range(GQA_GROUP):
            h = j * GQA_GROUP + g
            cols = slice(h * HEAD_DIM, (h + 1) * HEAD_DIM)
            gate = _silu(_gate_cols((g0_ref, g1_ref), h))
            o_ref[:, cols] = (o[g * blk:(g + 1) * blk, :] * gate).astype(o_ref.dtype)


def _lat_attention(z, qk, ck, cv, sink, layer, batch, seq_len, d_mix):
    t = z.shape[0]
    blk = ATT_BLOCK
    nb = seq_len // blk
    gw = ATT_WIDTH // 2
    past = ck.shape[2]
    kcol = OFF_AK // KV_WIDTH
    vcol = OFF_AV // KV_WIDTH

    def prev(b, n):
        return b * nb + jnp.maximum(n - 1, 0)

    def cur(b, n):
        return b * nb + n

    def nxt(b, n):
        return b * nb + jnp.minimum(n + 1, nb - 1)

    kern = functools.partial(_lat_attn_kernel, layer=layer, n_blocks=nb)
    return pl.pallas_call(
        kern,
        grid=(batch, nb),
        in_specs=[
            pl.BlockSpec(memory_space=pltpu.SMEM),
            pl.BlockSpec((blk, ATT_WIDTH), lambda b, n: (cur(b, n), 0)),
            pl.BlockSpec((blk, KV_WIDTH), lambda b, n: (prev(b, n), kcol)),
            pl.BlockSpec((blk, KV_WIDTH), lambda b, n: (cur(b, n), kcol)),
            pl.BlockSpec((blk, KV_WIDTH), lambda b, n: (nxt(b, n), kcol)),
            pl.BlockSpec((blk, KV_WIDTH), lambda b, n: (prev(b, n), vcol)),
            pl.BlockSpec((blk, KV_WIDTH), lambda b, n: (cur(b, n), vcol)),
            pl.BlockSpec((blk, KV_WIDTH), lambda b, n: (nxt(b, n), vcol)),
            pl.BlockSpec((None, None, past, KV_WIDTH), lambda b, n: (b, layer, 0, 0)),
            pl.BlockSpec((None, None, past, KV_WIDTH), lambda b, n: (b, layer, 0, 0)),
            pl.BlockSpec((blk, gw), lambda b, n: (cur(b, n), OFF_AG // gw)),
            pl.BlockSpec((blk, gw), lambda b, n: (cur(b, n), OFF_AG // gw + 1)),
        ],
        out_specs=pl.BlockSpec((blk, ATT_WIDTH), lambda b, n: (cur(b, n), 0)),
        out_shape=jax.ShapeDtypeStruct((t, d_mix), BF16),
        compiler_params=_cparams(("parallel", "parallel")),
        name="lat_attention",
    )(sink, qk, qk, qk, qk, z, z, z, ck, cv, z, z)


def _hgrn_level_table(c):
    t = np.arange(c)[:, None]
    s = np.arange(c)[None, :]
    x = (t // HG_DIAG) ^ (s // HG_DIAG)
    lvl = np.where(x == 0, 0, np.floor(np.log2(np.maximum(x, 1))).astype(np.int64) + 1)
    return np.where(s <= t, lvl, -1).astype(np.int32)


def _block_rows(g, blk, row_in_blk):
    c, w = g.shape
    pieces = [jnp.broadcast_to(g[p * blk + row_in_blk:p * blk + row_in_blk + 1, :], (blk, w))
              for p in range(c // blk)]
    return pieces[0] if len(pieces) == 1 else jnp.concatenate(pieces, axis=0)


def _hgrn_chunk_dir(q, kk, logf, vb, vtb, st_prev, tri, lvl, reverse):
    c = q.shape[0]
    h1 = logf.astype(BF16)
    r1 = logf - h1.astype(F32)
    h2 = r1.astype(BF16)
    h3 = (r1 - h2.astype(F32)).astype(BF16)
    gcum = (jnp.dot(tri, h1, preferred_element_type=F32) + jnp.dot(tri, h2, preferred_element_type=F32)
            + jnp.dot(tri, h3, preferred_element_type=F32))

    mid = HG_DIAG // 2 if reverse else HG_DIAG // 2 - 1
    d = gcum - _block_rows(gcum, HG_DIAG, mid)
    a = _dot_nt((q * jnp.exp(d)).astype(BF16), (kk * jnp.exp(-d)).astype(BF16))
    a = jnp.where(lvl == 0, a, 0.0)
    half = HG_DIAG
    k = 1
    while half < c:
        ref_row = half if reverse else half - 1
        e = jnp.exp(-jnp.abs(gcum - _block_rows(gcum, 2 * half, ref_row)))
        a_k = _dot_nt((q * e).astype(BF16), (kk * e).astype(BF16))
        a = jnp.where(lvl == k, a_k, a)
        half *= 2
        k += 1

    o = jnp.dot(a.astype(BF16), vb, preferred_element_type=F32)
    o += _dot_nt((q * jnp.exp(gcum)).astype(BF16), st_prev.astype(BF16))
    last = gcum[0:1, :] if reverse else gcum[c - 1:c, :]
    kd = (kk * jnp.exp(last - gcum)).astype(BF16)
    st_new = st_prev * jnp.exp(last) + jnp.dot(vtb, kd, preferred_element_type=F32)
    return o, st_new


def _chunk_rows(ci, chunk):
    if isinstance(ci, int):
        return pl.ds(ci * chunk, chunk)
    return pl.ds(pl.multiple_of(ci * chunk, chunk), chunk)


def _hgrn_kernel(*refs, layer, n_chunks, chunk, has_s0, want_state, n_alias_in):
    hq_ref, hi_ref, hff_ref, hfb_ref, hgt_ref, lb_ref, nw_ref, tri_ref, lvl_ref = refs[:9]
    pos = 9
    s0_ref = None
    if has_s0:
        s0_ref = refs[pos]
        pos += 1
    pos += n_alias_in
    out_ref = refs[pos]
    pos += 1
    st_ref = None
    if want_state:
        st_ref = refs[pos]
        pos += 1
    of_ref, ob_ref, sf_ref, sb_ref = refs[pos:pos + 4]

    lbs = []
    for dirn in range(2):
        raw = lb_ref[dirn]
        ex = jnp.exp(raw - jnp.max(raw, axis=0, keepdims=True))
        sm = ex / jnp.sum(ex, axis=0, keepdims=True)
        lb = jnp.zeros((1, HG_DK), F32)
        for l in range(1, layer + 1):
            lb = lb + sm[l:l + 1, :]
        lbs.append(lb)

    if has_s0:
        sf_ref[...] = s0_ref[0].T
        sb_ref[...] = s0_ref[1].T
    else:
        sf_ref[...] = jnp.zeros_like(sf_ref)
        sb_ref[...] = jnp.zeros_like(sb_ref)

    def gates(hf, lb):
        one_minus_f = (1.0 - lb) * jax.nn.sigmoid(-hf)
        return one_minus_f, jnp.log1p(-jnp.minimum(one_minus_f, 1.0 - GATE_EPS))

    def load(ci):
        rows = _chunk_rows(ci, chunk)
        q = _silu(hq_ref[rows, :]) * (HG_DK ** -0.5)
        v = hi_ref[rows, :]
        return rows, q, v.astype(BF16), v.T.astype(BF16)

    def step(i, carry):
        rows, q, vb, vtb = load(i)
        kf, gf = gates(hff_ref[rows, :], lbs[0])
        o, st = _hgrn_chunk_dir(q, kf, gf, vb, vtb, sf_ref[...], tri_ref[0], lvl_ref[0], False)
        of_ref[rows, :] = o
        sf_ref[...] = st
        rows, q, vb, vtb = load(n_chunks - 1 - i)
        kb, gb = gates(hfb_ref[rows, :], lbs[1])
        o, st = _hgrn_chunk_dir(q, kb, gb, vb, vtb, sb_ref[...], tri_ref[1], lvl_ref[1], True)
        ob_ref[rows, :] = o
        sb_ref[...] = st
        return carry

    if n_chunks == 1:
        step(0, 0)
    else:
        lax.fori_loop(0, n_chunks, step, 0)

    def finish(i, carry):
        rows = _chunk_rows(i, chunk)
        o = of_ref[rows, :] + ob_ref[rows, :]
        o = o * lax.rsqrt(jnp.mean(o * o, axis=-1, keepdims=True) + 1e-6) * nw_ref[...]
        out_ref[rows, :] = (o * _silu(hgt_ref[rows, :])).astype(out_ref.dtype)
        return carry

    if n_chunks == 1:
        finish(0, 0)
    else:
        lax.fori_loop(0, n_chunks, finish, 0)

    if want_state:
        st_ref[0] = sf_ref[...].T
        st_ref[1] = sb_ref[...].T


def _hgrn(z, mix, lb_raw, norm_w, s0, layer, batch, seq_len, want_state, states=None):
    t = z.shape[0]
    chunk = min(HG_CHUNK, seq_len)
    n_chunks = seq_len // chunk
    depth = norm_w.shape[0]
    has_s0 = s0 is not None
    lvl_f = _hgrn_level_table(chunk)
    lvl = jnp.asarray(np.stack([lvl_f, lvl_f.T]))
    low = np.tril(np.ones((chunk, chunk), np.float32))
    tri = jnp.asarray(np.stack([low, low.T]), dtype=BF16)

    def zcol(off):
        return pl.BlockSpec((seq_len, HG_DK), lambda b, h: (b, off // HG_DK + h))

    in_specs = [zcol(OFF_HQ), zcol(OFF_HI), zcol(OFF_HFF), zcol(OFF_HFB), zcol(OFF_HG),
                pl.BlockSpec((2, depth, HG_DK), lambda b, h: (0, 0, h)),
                pl.BlockSpec((None, 1, HG_DV), lambda b, h: (layer, 0, h)),
                pl.BlockSpec((2, chunk, chunk), lambda b, h: (0, 0, 0)),
                pl.BlockSpec((2, chunk, chunk), lambda b, h: (0, 0, 0))]
    args = [z, z, z, z, z, lb_raw, norm_w.reshape(depth, 1, HG_WIDTH), tri, lvl]
    if has_s0:
        in_specs.append(pl.BlockSpec((None, None, 2, None, HG_DK, HG_DV), lambda b, h: (b, layer, 0, h, 0, 0)))
        args.append(s0)
    aliases = {len(args): 0}
    in_specs.append(pl.BlockSpec(memory_space=pl.ANY))
    args.append(mix)
    out_specs = [pl.BlockSpec((seq_len, HG_DV), lambda b, h: (b, ATT_WIDTH // HG_DV + h))]
    out_shape = [jax.ShapeDtypeStruct(mix.shape, mix.dtype)]
    if want_state:
        out_specs.append(pl.BlockSpec((None, None, 2, None, HG_DK, HG_DV), lambda b, h: (b, layer, 0, h, 0, 0)))
        out_shape.append(jax.ShapeDtypeStruct((batch, depth, 2, HG_HEADS, HG_DK, HG_DV), F32))
        if states is not None:
            aliases[len(args)] = 1
            in_specs.append(pl.BlockSpec(memory_space=pl.ANY))
            args.append(states)
    kern = functools.partial(_hgrn_kernel, layer=layer, n_chunks=n_chunks, chunk=chunk,
                             has_s0=has_s0, want_state=want_state, n_alias_in=len(aliases))
    res = pl.pallas_call(
        kern,
        grid=(batch, HG_HEADS),
        in_specs=in_specs,
        out_specs=out_specs,
        out_shape=out_shape,
        input_output_aliases=aliases,
        scratch_shapes=[pltpu.VMEM((seq_len, HG_DV), F32), pltpu.VMEM((seq_len, HG_DV), F32),
                        pltpu.VMEM((HG_DV, HG_DK), F32), pltpu.VMEM((HG_DV, HG_DK), F32)],
        compiler_params=_cparams(("parallel", "parallel")),
        name="hgrn",
    )(*args)
    return (res[0], res[1]) if want_state else (res[0], None)


def _conv_kernel(ap_ref, ac_ref, an_ref, bp_ref, bc_ref, bn_ref, g_ref, w_ref, b_ref, lnw_ref, lnb_ref,
                 mix_ref, o_ref, u_ref, y_ref, *, tiles_per_seq):
    del mix_ref
    i = pl.program_id(0) % tiles_per_seq
    tr = ac_ref.shape[0]
    halo = ap_ref.shape[0]
    keep_prev = jnp.where(i > 0, 1.0, 0.0)
    keep_next = jnp.where(i < tiles_per_seq - 1, 1.0, 0.0)
    u_ref[0:halo, :] = ap_ref[...] * jax.nn.sigmoid(bp_ref[...]) * keep_prev
    u_ref[halo:halo + tr, :] = ac_ref[...] * jax.nn.sigmoid(bc_ref[...])
    u_ref[halo + tr:, :] = an_ref[...] * jax.nn.sigmoid(bn_ref[...]) * keep_next

    lanes = 128
    base = halo - CONV_TAPS // 2
    for cgrp in range(CV_WIDTH // lanes):
        cols = slice(cgrp * lanes, (cgrp + 1) * lanes)
        acc = jnp.zeros((tr, lanes), F32)
        for tap in range(CONV_TAPS):
            acc = acc + u_ref[base + tap:base + tap + tr, cols] * w_ref[tap:tap + 1, cols]
        y_ref[:, cols] = acc + b_ref[:, cols]

    y = y_ref[...]
    mu = jnp.mean(y, axis=-1, keepdims=True)
    yc = y - mu
    var = jnp.mean(yc * yc, axis=-1, keepdims=True)
    yn = yc * lax.rsqrt(var + 1e-5) * lnw_ref[...] + lnb_ref[...]
    o_ref[...] = (_silu(yn) * _silu(g_ref[...])).astype(o_ref.dtype)


def _conv(z, mix, conv_w, conv_b, ln_w, ln_b, layer, seq_len):
    t = z.shape[0]
    depth = conv_w.shape[0]
    tr = min(CONV_TR, seq_len)
    halo = CONV_HALO
    tiles_per_seq = seq_len // tr
    hb = tr // halo
    n_halo = t // halo
    ca = OFF_CA // CV_WIDTH
    cb = OFF_CB // CV_WIDTH
    cg = OFF_CG // CV_WIDTH

    def prev(i):
        return jnp.maximum(i * hb - 1, 0)

    def nxt(i):
        return jnp.minimum((i + 1) * hb, n_halo - 1)

    kern = functools.partial(_conv_kernel, tiles_per_seq=tiles_per_seq)
    vec = lambda a: a.reshape(depth, 1, CV_WIDTH)
    vspec = pl.BlockSpec((None, 1, CV_WIDTH), lambda i: (layer, 0, 0))
    return pl.pallas_call(
        kern,
        grid=(t // tr,),
        in_specs=[
            pl.BlockSpec((halo, CV_WIDTH), lambda i: (prev(i), ca)),
            pl.BlockSpec((tr, CV_WIDTH), lambda i: (i, ca)),
            pl.BlockSpec((halo, CV_WIDTH), lambda i: (nxt(i), ca)),
            pl.BlockSpec((halo, CV_WIDTH), lambda i: (prev(i), cb)),
            pl.BlockSpec((tr, CV_WIDTH), lambda i: (i, cb)),
            pl.BlockSpec((halo, CV_WIDTH), lambda i: (nxt(i), cb)),
            pl.BlockSpec((tr, CV_WIDTH), lambda i: (i, cg)),
            pl.BlockSpec((None, CONV_TAPS, CV_WIDTH), lambda i: (layer, 0, 0)),
            vspec, vspec, vspec,
            pl.BlockSpec(memory_space=pl.ANY),
        ],
        out_specs=pl.BlockSpec((tr, CV_WIDTH), lambda i: (i, (ATT_WIDTH + HG_WIDTH) // CV_WIDTH)),
        out_shape=jax.ShapeDtypeStruct(mix.shape, mix.dtype),
        input_output_aliases={11: 0},
        scratch_shapes=[pltpu.VMEM((tr + 2 * halo, CV_WIDTH), F32), pltpu.VMEM((tr, CV_WIDTH), F32)],
        compiler_params=_cparams(("parallel",)),
        name="conv",
    )(z, z, z, z, z, z, z, conv_w, vec(conv_b), vec(ln_w), vec(ln_b), mix)


def kernel(x_prompt, x_sample, cache_k, cache_v, state_hgrn, c, c_ctx, ada_w, ada_b, w_in, w_out,
           attn_sink, hg_lower_bounds, hg_norm_w, conv_w, conv_b, conv_ln_w, conv_ln_b, ln_w, ln_b):
    batch, seq, d = x_prompt.shape
    dec_batch, dec_seq, _ = x_sample.shape
    depth = w_in.shape[0]
    past = cache_k.shape[2]
    alpha = float((2 * depth) ** 0.25)
    assert dec_batch + 1 <= MOD_ROWS and w_in.shape[2] == D_IN

    conds = jnp.concatenate([c_ctx[None, :], c, jnp.zeros((MOD_ROWS - 1 - dec_batch, d), F32)], axis=0)
    mod = _modulation(conds, ada_w, ada_b)
    w_in_bf = w_in.astype(BF16)
    w_out_bf = w_out.astype(BF16)
    cos, sin = _rope_tables(dec_seq)
    ck = cache_k.reshape(dec_batch, depth, past, KV_WIDTH)
    cv = cache_v.reshape(dec_batch, depth, past, KV_WIDTH)

    y = x_prompt.reshape(batch * seq, d)
    caches = states = None
    for l in range(depth):
        z, caches = _in_proj(y, mod, w_in_bf, l, 0, seq, caches, want_cache=True)
        mix = _ctx_attention(z, attn_sink, l, batch, seq, d)
        mix, states = _hgrn(z, mix, hg_lower_bounds, hg_norm_w, None, l, batch, seq, True, states)
        mix = _conv(z, mix, conv_w, conv_b, conv_ln_w, conv_ln_b, l, seq)
        y = _out_proj(mix, y, mod, w_out_bf, ln_w, ln_b, l, 0, seq, alpha)
    y_prompt = y.reshape(batch, seq, d)
    new_cache_k = caches[0].reshape(batch, depth, seq, ATT_KV_HEADS, HEAD_DIM)
    new_cache_v = caches[1].reshape(batch, depth, seq, ATT_KV_HEADS, HEAD_DIM)

    y = x_sample.reshape(dec_batch * dec_seq, d)
    for l in range(depth):
        z, _ = _in_proj(y, mod, w_in_bf, l, 1, dec_seq)
        qk = _rope(z, cos, sin, dec_seq)
        mix = _lat_attention(z, qk, ck, cv, attn_sink, l, dec_batch, dec_seq, d)
        mix, _ = _hgrn(z, mix, hg_lower_bounds, hg_norm_w, state_hgrn, l, dec_batch, dec_seq, False)
        mix = _conv(z, mix, conv_w, conv_b, conv_ln_w, conv_ln_b, l, dec_seq)
        y = _out_proj(mix, y, mod, w_out_bf, ln_w, ln_b, l, 1, dec_seq, alpha)
    y_sample = y.reshape(dec_batch, dec_seq, d)

    return (y_prompt, y_sample, new_cache_k, new_cache_v, states)
```

```python
import functools

import numpy as np
import jax
import jax.numpy as jnp
from jax import lax
from jax.experimental import pallas as pl
from jax.experimental.pallas import tpu as pltpu

F32 = jnp.float32
BF16 = jnp.bfloat16

ATT_HEADS = 8
ATT_KV_HEADS = 2
GQA_GROUP = ATT_HEADS // ATT_KV_HEADS
HEAD_DIM = 128
ATT_WIDTH = ATT_HEADS * HEAD_DIM
KV_WIDTH = ATT_KV_HEADS * HEAD_DIM
WINDOW = 128
ATT_BLOCK = 128
ROPE_BASE = 10000.0
GRID_W = 64
HG_HEADS = 4
HG_DK = 128
HG_DV = 128
HG_WIDTH = HG_HEADS * HG_DV
GATE_EPS = 1e-6
CV_WIDTH = 512
CONV_TAPS = 31
NEG = -1e30

OFF_AQ = 0
OFF_AK = OFF_AQ + ATT_WIDTH
OFF_AV = OFF_AK + KV_WIDTH
OFF_AG = OFF_AV + KV_WIDTH
OFF_HQ = OFF_AG + ATT_WIDTH
OFF_HI = OFF_HQ + HG_WIDTH
OFF_HFF = OFF_HI + HG_WIDTH
OFF_HFB = OFF_HFF + HG_WIDTH
OFF_HG = OFF_HFB + HG_WIDTH
OFF_CA = OFF_HG + HG_WIDTH
OFF_CB = OFF_CA + CV_WIDTH
OFF_CG = OFF_CB + CV_WIDTH
D_IN = OFF_CG + CV_WIDTH

MOD_ROWS = 8
MOD_TN = 512
IN_TM = 1024
IN_TN = 1664
OUT_TM = 512
HG_CHUNK = 256
HG_UNROLL = 2
HG_DIAG = 8
CONV_TR = 256
CONV_HALO = 16
CONV_ROW_BLOCK = 64
VMEM_LIMIT = 56 * 1024 * 1024


def _cparams(sem):
    return pltpu.CompilerParams(dimension_semantics=sem, vmem_limit_bytes=VMEM_LIMIT)


def _silu(x):
    return x * jax.nn.sigmoid(x)


def _pick_row(ref, r):
    v = ref[...]
    rows = lax.broadcasted_iota(jnp.int32, v.shape, 0)
    return jnp.sum(jnp.where(rows == r, v, 0.0), axis=0, keepdims=True)


def _dot_nt(a, b):
    return lax.dot_general(a, b, (((1,), (1,)), ((), ())), preferred_element_type=F32)


def _mod_kernel(cond_ref, w_ref, b_ref, o_ref):
    s = _silu(cond_ref[...]).astype(BF16)
    o_ref[...] = jnp.dot(s, w_ref[...].astype(BF16), preferred_element_type=F32) + b_ref[...]


def _modulation(conds, ada_w, ada_b):
    depth, d, n = ada_w.shape
    return pl.pallas_call(
        _mod_kernel,
        grid=(depth, n // MOD_TN),
        in_specs=[
            pl.BlockSpec((MOD_ROWS, d), lambda l, j: (0, 0)),
            pl.BlockSpec((None, d, MOD_TN), lambda l, j: (l, 0, j)),
            pl.BlockSpec((None, 1, MOD_TN), lambda l, j: (l, 0, j)),
        ],
        out_specs=pl.BlockSpec((None, MOD_ROWS, MOD_TN), lambda l, j: (l, 0, j)),
        out_shape=jax.ShapeDtypeStruct((depth, MOD_ROWS, n), F32),
        compiler_params=_cparams(("parallel", "parallel")),
        name="modulation",
    )(conds, ada_w, ada_b.reshape(depth, 1, n))


def _in_proj_kernel(*refs, row0, tiles_per_seq, kv_tile, kv_col, n_cache_in):
    x_ref, shift_ref, scale_ref, w_ref = refs[:4]
    outs = refs[4 + n_cache_in:]
    z_ref, xb_ref = outs[0], outs[-1]
    i = pl.program_id(0)
    j = pl.program_id(1)

    @pl.when(j == 0)
    def _():
        r = row0 + i // tiles_per_seq
        sh = _pick_row(shift_ref, r)
        sc = _pick_row(scale_ref, r)
        xb_ref[...] = (x_ref[...] * (1.0 + sc) + sh).astype(BF16)

    z_ref[...] = jnp.dot(xb_ref[...], w_ref[...], preferred_element_type=F32)

    if kv_tile is not None:
        kc_ref, vc_ref = outs[1], outs[2]

        @pl.when(j == kv_tile)
        def _():
            kc_ref[...] = z_ref[:, kv_col:kv_col + KV_WIDTH].reshape(kc_ref.shape)
            vc_ref[...] = z_ref[:, kv_col + KV_WIDTH:kv_col + 2 * KV_WIDTH].reshape(vc_ref.shape)


def _in_proj(x2, mod, w_in_bf, layer, row0, seq_len, caches=None, want_cache=False):
    t, d = x2.shape
    depth = w_in_bf.shape[0]
    tm = min(IN_TM, t)
    tn = IN_TN
    tiles_per_seq = max(seq_len // tm, 1) if row0 else t // tm + 1
    in_specs = [
        pl.BlockSpec((tm, d), lambda i, j: (i, 0)),
        pl.BlockSpec((None, MOD_ROWS, d), lambda i, j: (layer, 0, 0)),
        pl.BlockSpec((None, MOD_ROWS, d), lambda i, j: (layer, 0, 1)),
        pl.BlockSpec((None, d, tn), lambda i, j: (layer, 0, j)),
    ]
    args = [x2, mod, mod, w_in_bf]
    out_specs = [pl.BlockSpec((tm, tn), lambda i, j: (i, j))]
    out_shape = [jax.ShapeDtypeStruct((t, D_IN), F32)]
    aliases = {}
    kv_tile = kv_col = None
    n_cache_in = 0
    if want_cache:
        kv_tile = OFF_AK // tn
        kv_col = OFF_AK - kv_tile * tn
        assert kv_col + 2 * KV_WIDTH <= tn and tm % seq_len == 0 and OFF_AV == OFF_AK + KV_WIDTH
        cshape = (t // seq_len, depth, seq_len, KV_WIDTH)
        cspec = pl.BlockSpec((tm // seq_len, None, seq_len, KV_WIDTH), lambda i, j: (i, layer, 0, 0))
        out_specs += [cspec, cspec]
        out_shape += [jax.ShapeDtypeStruct(cshape, F32)] * 2
        if caches is not None:
            n_cache_in = 2
            in_specs += [pl.BlockSpec(memory_space=pl.ANY)] * 2
            args += list(caches)
            aliases = {4: 1, 5: 2}
    kern = functools.partial(_in_proj_kernel, row0=row0, tiles_per_seq=tiles_per_seq,
                             kv_tile=kv_tile, kv_col=kv_col, n_cache_in=n_cache_in)
    res = pl.pallas_call(
        kern,
        grid=(t // tm, D_IN // tn),
        in_specs=in_specs,
        out_specs=out_specs,
        out_shape=out_shape,
        input_output_aliases=aliases,
        scratch_shapes=[pltpu.VMEM((tm, d), BF16)],
        compiler_params=_cparams(("parallel", "arbitrary")),
        name="in_proj",
    )(*args)
    return (res[0], (res[1], res[2])) if want_cache else (res[0], None)


def _out_proj_kernel(mix_ref, x_ref, gate_ref, w_ref, lnw_ref, lnb_ref, y_ref,
                     *, row0, tiles_per_seq, alpha):
    r = row0 + pl.program_id(0) // tiles_per_seq
    gate = _pick_row(gate_ref, r)
    acc = jnp.dot(mix_ref[...], w_ref[...], preferred_element_type=F32)
    y = alpha * x_ref[...] + gate * acc
    mu = jnp.mean(y, axis=-1, keepdims=True)
    yc = y - mu
    var = jnp.mean(yc * yc, axis=-1, keepdims=True)
    y_ref[...] = yc * lax.rsqrt(var + 1e-5) * lnw_ref[...] + lnb_ref[...]


def _out_proj(mix, x2, mod, w_out_bf, ln_w, ln_b, layer, row0, seq_len, alpha):
    t, d = x2.shape
    tm = min(OUT_TM, t)
    tiles_per_seq = max(seq_len // tm, 1) if row0 else t // tm + 1
    depth = w_out_bf.shape[0]
    kern = functools.partial(_out_proj_kernel, row0=row0, tiles_per_seq=tiles_per_seq, alpha=alpha)
    return pl.pallas_call(
        kern,
        grid=(t // tm,),
        in_specs=[
            pl.BlockSpec((tm, mix.shape[1]), lambda i: (i, 0)),
            pl.BlockSpec((tm, d), lambda i: (i, 0)),
            pl.BlockSpec((None, MOD_ROWS, d), lambda i: (layer, 0, 2)),
            pl.BlockSpec((None, d, d), lambda i: (layer, 0, 0)),
            pl.BlockSpec((None, 1, d), lambda i: (layer, 0, 0)),
            pl.BlockSpec((None, 1, d), lambda i: (layer, 0, 0)),
        ],
        out_specs=pl.BlockSpec((tm, d), lambda i: (i, 0)),
        out_shape=jax.ShapeDtypeStruct((t, d), F32),
        compiler_params=_cparams(("parallel",)),
        name="out_proj",
    )(mix, x2, mod, w_out_bf, ln_w.reshape(depth, 1, d), ln_b.reshape(depth, 1, d))


def _gate_cols(g_refs, h):
    per = g_refs[0].shape[1] // HEAD_DIM
    return g_refs[h // per][:, (h % per) * HEAD_DIM:(h % per + 1) * HEAD_DIM]


def _ctx_attn_kernel(sink_ref, q_ref, kv_ref, g0_ref, g1_ref, o_ref, *, layer):
    scale = HEAD_DIM ** -0.5
    for j in range(ATT_KV_HEADS):
        kb = kv_ref[:, j * HEAD_DIM:(j + 1) * HEAD_DIM].astype(BF16)
        vb = kv_ref[:, KV_WIDTH + j * HEAD_DIM:KV_WIDTH + (j + 1) * HEAD_DIM].astype(BF16)
        for g in range(GQA_GROUP):
            h = j * GQA_GROUP + g
            cols = slice(h * HEAD_DIM, (h + 1) * HEAD_DIM)
            sink = sink_ref[layer, h]
            s = _dot_nt(q_ref[:, cols].astype(BF16), kb) * scale
            m = jnp.maximum(jnp.max(s, axis=-1, keepdims=True), sink)
            p = jnp.exp(s - m)
            denom = jnp.sum(p, axis=-1, keepdims=True) + jnp.exp(sink - m)
            o = jnp.dot(p.astype(BF16), vb, preferred_element_type=F32) / denom
            o_ref[:, cols] = (o * _silu(_gate_cols((g0_ref, g1_ref), h))).astype(o_ref.dtype)


def _ctx_attention(z, sink, layer, batch, seq_len, d_mix):
    t = z.shape[0]
    gw = ATT_WIDTH // 2
    kern = functools.partial(_ctx_attn_kernel, layer=layer)
    return pl.pallas_call(
        kern,
        grid=(batch,),
        in_specs=[
            pl.BlockSpec(memory_space=pltpu.SMEM),
            pl.BlockSpec((seq_len, ATT_WIDTH), lambda b: (b, OFF_AQ // ATT_WIDTH)),
            pl.BlockSpec((seq_len, 2 * KV_WIDTH), lambda b: (b, OFF_AK // (2 * KV_WIDTH))),
            pl.BlockSpec((seq_len, gw), lambda b: (b, OFF_AG // gw)),
            pl.BlockSpec((seq_len, gw), lambda b: (b, OFF_AG // gw + 1)),
        ],
        out_specs=pl.BlockSpec((seq_len, ATT_WIDTH), lambda b: (b, 0)),
        out_shape=jax.ShapeDtypeStruct((t, d_mix), BF16),
        compiler_params=_cparams(("parallel",)),
        name="ctx_attention",
    )(sink, z, z, z, z)


def _rope_tables(seq_len):
    pos = np.arange(seq_len)
    nf = HEAD_DIM // 4
    inv = ROPE_BASE ** (-np.arange(nf, dtype=np.float64) / nf)
    ang_r = (pos // GRID_W)[:, None] * inv[None, :]
    ang_c = (pos % GRID_W)[:, None] * inv[None, :]
    cos = np.concatenate([np.cos(ang_r), np.cos(ang_r), np.cos(ang_c), np.cos(ang_c)], axis=-1)
    sin = np.concatenate([-np.sin(ang_r), np.sin(ang_r), -np.sin(ang_c), np.sin(ang_c)], axis=-1)
    return jnp.asarray(cos, F32), jnp.asarray(sin, F32)


def _rope_kernel(x_ref, cos_ref, sin_ref, o_ref):
    cos = cos_ref[...]
    sin = sin_ref[...]
    nf = HEAD_DIM // 4
    lane = lax.broadcasted_iota(jnp.int32, cos.shape, 1)
    first = (lane % (2 * nf)) < nf
    for h in range(x_ref.shape[1] // HEAD_DIM):
        cols = slice(h * HEAD_DIM, (h + 1) * HEAD_DIM)
        x = x_ref[:, cols]
        partner = jnp.where(first, pltpu.roll(x, HEAD_DIM - nf, 1), pltpu.roll(x, nf, 1))
        o_ref[:, cols] = (x * cos + partner * sin).astype(o_ref.dtype)


def _rope(z, cos, sin, seq_len):
    t = z.shape[0]
    tr = 256
    width = ATT_WIDTH + KV_WIDTH
    per_seq = seq_len // tr
    return pl.pallas_call(
        _rope_kernel,
        grid=(t // tr,),
        in_specs=[
            pl.BlockSpec((tr, width), lambda i: (i, 0)),
            pl.BlockSpec((tr, HEAD_DIM), lambda i: (i % per_seq, 0)),
            pl.BlockSpec((tr, HEAD_DIM), lambda i: (i % per_seq, 0)),
        ],
        out_specs=pl.BlockSpec((tr, width), lambda i: (i, 0)),
        out_shape=jax.ShapeDtypeStruct((t, width), BF16),
        compiler_params=_cparams(("parallel",)),
        name="rope",
    )(z, cos, sin)


def _lat_attn_kernel(sink_ref, q_ref, kp_ref, kc_ref, kn_ref, vp_ref, vc_ref, vn_ref, ck_ref, cv_ref,
                     g0_ref, g1_ref, o_ref, *, layer, n_blocks):
    n = pl.program_id(1)
    blk = ATT_BLOCK
    past = ck_ref.shape[0]
    scale = HEAD_DIM ** -0.5
    nkeys = 3 * blk + past
    qi = lax.broadcasted_iota(jnp.int32, (blk, nkeys), 0)
    col = lax.broadcasted_iota(jnp.int32, (blk, nkeys), 1)
    prev_lo = jnp.where(n > 0, 0, blk)
    next_hi = jnp.where(n < n_blocks - 1, 3 * blk, 2 * blk)
    ok_prev = (col < blk) & (col >= qi) & (col >= prev_lo)
    ok_next = (col >= 2 * blk) & (col < next_hi) & (col - 2 * blk <= qi)
    ok = ok_prev | ((col >= blk) & (col < 2 * blk)) | ok_next | (col >= 3 * blk)
    bias = jnp.where(ok, 0.0, NEG)
    bias = jnp.concatenate([bias] * GQA_GROUP, axis=0)
    rsel = lax.broadcasted_iota(jnp.int32, (GQA_GROUP * blk, 1), 0) // blk

    for j in range(ATT_KV_HEADS):
        hcols = slice(j * HEAD_DIM, (j + 1) * HEAD_DIM)
        kcat = jnp.concatenate([kp_ref[:, hcols], kc_ref[:, hcols], kn_ref[:, hcols],
                                ck_ref[:, hcols].astype(BF16)], axis=0)
        vcat = jnp.concatenate([vp_ref[:, hcols].astype(BF16), vc_ref[:, hcols].astype(BF16),
                                vn_ref[:, hcols].astype(BF16), cv_ref[:, hcols].astype(BF16)], axis=0)
        q4 = jnp.concatenate([q_ref[:, (j * GQA_GROUP + g) * HEAD_DIM:(j * GQA_GROUP + g + 1) * HEAD_DIM]
                              for g in range(GQA_GROUP)], axis=0)
        s = _dot_nt(q4, kcat) * scale + bias
        sink = jnp.zeros((GQA_GROUP * blk, 1), F32)
        for g in range(GQA_GROUP):
            sink = jnp.where(rsel == g, sink_ref[layer, j * GQA_GROUP + g], sink)
        m = jnp.maximum(jnp.max(s, axis=-1, keepdims=True), sink)
        p = jnp.exp(s - m)
        denom = jnp.sum(p, axis=-1, keepdims=True) + jnp.exp(sink - m)
        o = jnp.dot(p.astype(BF16), vcat, preferred_element_type=F32) / denom
        for g in range(GQA_GROUP):
            h = j * GQA_GROUP + g
            cols = slice(h * HEAD_DIM, (h + 1) * HEAD_DIM)
            gate = _silu(_gate_cols((g0_ref, g1_ref), h))
            o_ref[:, cols] = (o[g * blk:(g + 1) * blk, :] * gate).astype(o_ref.dtype)


def _lat_attention(z, qk, ck, cv, sink, layer, batch, seq_len, d_mix):
    t = z.shape[0]
    blk = ATT_BLOCK
    nb = seq_len // blk
    gw = ATT_WIDTH // 2
    past = ck.shape[2]
    kcol = OFF_AK // KV_WIDTH
    vcol = OFF_AV // KV_WIDTH

    def prev(b, n):
        return b * nb + jnp.maximum(n - 1, 0)

    def cur(b, n):
        return b * nb + n

    def nxt(b, n):
        return b * nb + jnp.minimum(n + 1, nb - 1)

    kern = functools.partial(_lat_attn_kernel, layer=layer, n_blocks=nb)
    return pl.pallas_call(
        kern,
        grid=(batch, nb),
        in_specs=[
            pl.BlockSpec(memory_space=pltpu.SMEM),
            pl.BlockSpec((blk, ATT_WIDTH), lambda b, n: (cur(b, n), 0)),
            pl.BlockSpec((blk, KV_WIDTH), lambda b, n: (prev(b, n), kcol)),
            pl.BlockSpec((blk, KV_WIDTH), lambda b, n: (cur(b, n), kcol)),
            pl.BlockSpec((blk, KV_WIDTH), lambda b, n: (nxt(b, n), kcol)),
            pl.BlockSpec((blk, KV_WIDTH), lambda b, n: (prev(b, n), vcol)),
            pl.BlockSpec((blk, KV_WIDTH), lambda b, n: (cur(b, n), vcol)),
            pl.BlockSpec((blk, KV_WIDTH), lambda b, n: (nxt(b, n), vcol)),
            pl.BlockSpec((None, None, past, KV_WIDTH), lambda b, n: (b, layer, 0, 0)),
            pl.BlockSpec((None, None, past, KV_WIDTH), lambda b, n: (b, layer, 0, 0)),
            pl.BlockSpec((blk, gw), lambda b, n: (cur(b, n), OFF_AG // gw)),
            pl.BlockSpec((blk, gw), lambda b, n: (cur(b, n), OFF_AG // gw + 1)),
        ],
        out_specs=pl.BlockSpec((blk, ATT_WIDTH), lambda b, n: (cur(b, n), 0)),
        out_shape=jax.ShapeDtypeStruct((t, d_mix), BF16),
        compiler_params=_cparams(("parallel", "parallel")),
        name="lat_attention",
    )(sink, qk, qk, qk, qk, z, z, z, ck, cv, z, z)


def _hgrn_level_table(c):
    t = np.arange(c)[:, None]
    s = np.arange(c)[None, :]
    x = (t // HG_DIAG) ^ (s // HG_DIAG)
    lvl = np.where(x == 0, 0, np.floor(np.log2(np.maximum(x, 1))).astype(np.int64) + 1)
    return np.where(s <= t, lvl, -1).astype(np.int32)


def _hgrn_n_levels(c):
    return int(np.log2(c // HG_DIAG))


def _hgrn_sum_matrix(c, reverse):
    idx = np.arange(c)
    tri = (idx[None, :] >= idx[:, None]) if reverse else (idx[None, :] <= idx[:, None])
    return np.concatenate([tri.astype(np.float32)] * 3, axis=1)


def _hgrn_sign_table(c, reverse):
    idx = np.arange(c)
    rows = []
    half = HG_DIAG
    while half < c:
        second = (idx % (2 * half)) >= half
        rows.append(np.where(second != reverse, 1.0, -1.0))
        half *= 2
    return np.broadcast_to(np.stack(rows)[:, :, None], (len(rows), c, HG_DK)).astype(np.float32)


def _block_rows(g, blk, row_in_blk):
    c, w = g.shape
    pieces = [jnp.broadcast_to(g[p * blk + row_in_blk:p * blk + row_in_blk + 1, :], (blk, w))
              for p in range(c // blk)]
    return pieces[0] if len(pieces) == 1 else jnp.concatenate(pieces, axis=0)


def _hgrn_chunk_dir(qb, kk, lg2, vb, vtb, st_prev, tri3, sgn_ref, lvl, reverse):
    c = qb.shape[0]
    h1 = lg2.astype(BF16)
    r1 = lg2 - h1.astype(F32)
    h2 = r1.astype(BF16)
    h3 = (r1 - h2.astype(F32)).astype(BF16)
    gcum = jnp.dot(tri3, jnp.concatenate([h1, h2, h3], axis=0), preferred_element_type=F32)
    kb = kk.astype(BF16)

    mid = HG_DIAG // 2 if reverse else HG_DIAG // 2 - 1
    d = gcum - _block_rows(gcum, HG_DIAG, mid)
    a = _dot_nt(qb * jnp.exp2(d).astype(BF16), kb * jnp.exp2(-d).astype(BF16))
    a = jnp.where(lvl == 0, a, 0.0)
    half = HG_DIAG
    k = 1
    while half < c:
        ref_row = half if reverse else half - 1
        e = jnp.exp2((gcum - _block_rows(gcum, 2 * half, ref_row)) * sgn_ref[k - 1]).astype(BF16)
        a = jnp.where(lvl == k, _dot_nt(qb * e, kb * e), a)
        half *= 2
        k += 1

    o = jnp.dot(a.astype(BF16), vb, preferred_element_type=F32)
    o += _dot_nt(qb * jnp.exp2(gcum).astype(BF16), st_prev.astype(BF16))
    last = gcum[0:1, :] if reverse else gcum[c - 1:c, :]
    kd = kb * jnp.exp2(last - gcum).astype(BF16)
    st_new = st_prev * jnp.exp2(last) + jnp.dot(vtb, kd, preferred_element_type=F32)
    return o, st_new


def _chunk_rows(ci, chunk):
    if isinstance(ci, int):
        return pl.ds(ci * chunk, chunk)
    return pl.ds(pl.multiple_of(ci * chunk, chunk), chunk)


def _hgrn_kernel(*refs, layer, n_chunks, chunk, has_s0, want_state, n_alias_in):
    hq_ref, hi_ref, hff_ref, hfb_ref, hgt_ref, lb_ref, nw_ref, tri_ref, sgn_ref, lvl_ref = refs[:10]
    pos = 10
    s0_ref = None
    if has_s0:
        s0_ref = refs[pos]
        pos += 1
    pos += n_alias_in
    out_ref = refs[pos]
    pos += 1
    st_ref = None
    if want_state:
        st_ref = refs[pos]
        pos += 1
    of_ref, ob_ref, sf_ref, sb_ref, qb_ref, vb_ref, vtb_ref = refs[pos:pos + 7]

    lbs = []
    for dirn in range(2):
        raw = lb_ref[dirn]
        ex = jnp.exp(raw - jnp.max(raw, axis=0, keepdims=True))
        sm = ex / jnp.sum(ex, axis=0, keepdims=True)
        lb = jnp.zeros((1, HG_DK), F32)
        for l in range(1, layer + 1):
            lb = lb + sm[l:l + 1, :]
        lbs.append(lb)

    if has_s0:
        sf_ref[...] = s0_ref[0].T
        sb_ref[...] = s0_ref[1].T
    else:
        sf_ref[...] = jnp.zeros_like(sf_ref)
        sb_ref[...] = jnp.zeros_like(sb_ref)

    unroll = min(HG_UNROLL, n_chunks)

    def loop(body):
        if n_chunks == unroll:
            for i in range(n_chunks):
                body(i, 0)
        else:
            def group(g, carry):
                for u in range(unroll):
                    body(g * unroll + u, carry)
                return carry
            lax.fori_loop(0, n_chunks // unroll, group, 0)

    def prepare(i, carry):
        rows = _chunk_rows(i, chunk)
        qb_ref[rows, :] = (_silu(hq_ref[rows, :]) * (HG_DK ** -0.5)).astype(BF16)
        v = hi_ref[rows, :]
        vb_ref[rows, :] = v.astype(BF16)
        vtb_ref[i] = v.T.astype(BF16)
        return carry

    loop(prepare)

    def gates(hf, lb):
        one_minus_f = (1.0 - lb) * jax.nn.sigmoid(-hf)
        return one_minus_f, jnp.log2(1.0 - jnp.minimum(one_minus_f, 1.0 - GATE_EPS))

    def one_dir(ci, hf_ref, dirn, o_ref, s_ref):
        rows = _chunk_rows(ci, chunk)
        kk, lg2 = gates(hf_ref[rows, :], lbs[dirn])
        o, st = _hgrn_chunk_dir(qb_ref[rows, :], kk, lg2, vb_ref[rows, :], vtb_ref[ci], s_ref[...],
                                tri_ref[dirn], sgn_ref.at[dirn], lvl_ref[dirn], dirn == 1)
        o_ref[rows, :] = o
        s_ref[...] = st

    def step(i, carry):
        one_dir(i, hff_ref, 0, of_ref, sf_ref)
        one_dir(n_chunks - 1 - i, hfb_ref, 1, ob_ref, sb_ref)
        return carry

    loop(step)

    def finish(i, carry):
        rows = _chunk_rows(i, chunk)
        o = of_ref[rows, :] + ob_ref[rows, :]
        o = o * lax.rsqrt(jnp.mean(o * o, axis=-1, keepdims=True) + 1e-6) * nw_ref[...]
        out_ref[rows, :] = (o * _silu(hgt_ref[rows, :])).astype(out_ref.dtype)
        return carry

    loop(finish)

    if want_state:
        st_ref[0] = sf_ref[...].T
        st_ref[1] = sb_ref[...].T


def _hgrn(z, mix, lb_raw, norm_w, s0, layer, batch, seq_len, want_state, states=None):
    t = z.shape[0]
    chunk = min(HG_CHUNK, seq_len)
    n_chunks = seq_len // chunk
    depth = norm_w.shape[0]
    has_s0 = s0 is not None
    lvl_f = _hgrn_level_table(chunk)
    lvl = jnp.asarray(np.stack([lvl_f, lvl_f.T]))
    tri = jnp.asarray(np.stack([_hgrn_sum_matrix(chunk, False), _hgrn_sum_matrix(chunk, True)]), dtype=BF16)
    sgn = jnp.asarray(np.stack([_hgrn_sign_table(chunk, False), _hgrn_sign_table(chunk, True)]))

    def zcol(off):
        return pl.BlockSpec((seq_len, HG_DK), lambda b, h: (b, off // HG_DK + h))

    in_specs = [zcol(OFF_HQ), zcol(OFF_HI), zcol(OFF_HFF), zcol(OFF_HFB), zcol(OFF_HG),
                pl.BlockSpec((2, depth, HG_DK), lambda b, h: (0, 0, h)),
                pl.BlockSpec((None, 1, HG_DV), lambda b, h: (layer, 0, h)),
                pl.BlockSpec(tri.shape, lambda b, h: (0, 0, 0)),
                pl.BlockSpec(sgn.shape, lambda b, h: (0, 0, 0, 0)),
                pl.BlockSpec((2, chunk, chunk), lambda b, h: (0, 0, 0))]
    args = [z, z, z, z, z, lb_raw, norm_w.reshape(depth, 1, HG_WIDTH), tri, sgn, lvl]
    if has_s0:
        in_specs.append(pl.BlockSpec((None, None, 2, None, HG_DK, HG_DV), lambda b, h: (b, layer, 0, h, 0, 0)))
        args.append(s0)
    aliases = {len(args): 0}
    in_specs.append(pl.BlockSpec(memory_space=pl.ANY))
    args.append(mix)
    out_specs = [pl.BlockSpec((seq_len, HG_DV), lambda b, h: (b, ATT_WIDTH // HG_DV + h))]
    out_shape = [jax.ShapeDtypeStruct(mix.shape, mix.dtype)]
    if want_state:
        out_specs.append(pl.BlockSpec((None, None, 2, None, HG_DK, HG_DV), lambda b, h: (b, layer, 0, h, 0, 0)))
        out_shape.append(jax.ShapeDtypeStruct((batch, depth, 2, HG_HEADS, HG_DK, HG_DV), F32))
        if states is not None:
            aliases[len(args)] = 1
            in_specs.append(pl.BlockSpec(memory_space=pl.ANY))
            args.append(states)
    kern = functools.partial(_hgrn_kernel, layer=layer, n_chunks=n_chunks, chunk=chunk,
                             has_s0=has_s0, want_state=want_state, n_alias_in=len(aliases))
    res = pl.pallas_call(
        kern,
        grid=(batch, HG_HEADS),
        in_specs=in_specs,
        out_specs=out_specs,
        out_shape=out_shape,
        input_output_aliases=aliases,
        scratch_shapes=[pltpu.VMEM((seq_len, HG_DV), F32), pltpu.VMEM((seq_len, HG_DV), F32),
                        pltpu.VMEM((HG_DV, HG_DK), F32), pltpu.VMEM((HG_DV, HG_DK), F32),
                        pltpu.VMEM((seq_len, HG_DK), BF16), pltpu.VMEM((seq_len, HG_DV), BF16),
                        pltpu.VMEM((n_chunks, HG_DV, chunk), BF16)],
        compiler_params=_cparams(("parallel", "parallel")),
        name="hgrn",
    )(*args)
    return (res[0], res[1]) if want_state else (res[0], None)


def _conv_kernel(ap_ref, ac_ref, an_ref, bp_ref, bc_ref, bn_ref, g_ref, w_ref, b_ref, lnw_ref, lnb_ref,
                 mix_ref, o_ref, u_ref, y_ref, *, tiles_per_seq):
    del mix_ref
    i = pl.program_id(0) % tiles_per_seq
    tr = ac_ref.shape[0]
    halo = ap_ref.shape[0]
    keep_prev = jnp.where(i > 0, 1.0, 0.0)
    keep_next = jnp.where(i < tiles_per_seq - 1, 1.0, 0.0)
    u_ref[0:halo, :] = ap_ref[...] * jax.nn.sigmoid(bp_ref[...]) * keep_prev
    u_ref[halo:halo + tr, :] = ac_ref[...] * jax.nn.sigmoid(bc_ref[...])
    u_ref[halo + tr:, :] = an_ref[...] * jax.nn.sigmoid(bn_ref[...]) * keep_next

    lanes = 128
    sub = 8
    base = halo - CONV_TAPS // 2
    rb = CONV_ROW_BLOCK
    for cgrp in range(CV_WIDTH // lanes):
        cols = slice(cgrp * lanes, (cgrp + 1) * lanes)
        for r0 in range(0, tr, rb):
            y = None
            for s in range(sub):
                part = None
                for a in range((base + CONV_TAPS - 1) // sub + 1):
                    tap = sub * a + s - base
                    if 0 <= tap < CONV_TAPS:
                        term = u_ref[r0 + sub * a:r0 + sub * a + rb + sub, cols] * w_ref[tap:tap + 1, cols]
                        part = term if part is None else part + term
                shifted = part[s:s + rb, :]
                y = shifted if y is None else y + shifted
            y_ref[r0:r0 + rb, cols] = y + b_ref[:, cols]

    y = y_ref[...]
    mu = jnp.mean(y, axis=-1, keepdims=True)
    yc = y - mu
    var = jnp.mean(yc * yc, axis=-1, keepdims=True)
    yn = yc * lax.rsqrt(var + 1e-5) * lnw_ref[...] + lnb_ref[...]
    o_ref[...] = (_silu(yn) * _silu(g_ref[...])).astype(o_ref.dtype)


def _conv(z, mix, conv_w, conv_b, ln_w, ln_b, layer, seq_len):
    t = z.shape[0]
    depth = conv_w.shape[0]
    tr = min(CONV_TR, seq_len)
    halo = CONV_HALO
    tiles_per_seq = seq_len // tr
    hb = tr // halo
    n_halo = t // halo
    ca = OFF_CA // CV_WIDTH
    cb = OFF_CB // CV_WIDTH
    cg = OFF_CG // CV_WIDTH

    def prev(i):
        return jnp.maximum(i * hb - 1, 0)

    def nxt(i):
        return jnp.minimum((i + 1) * hb, n_halo - 1)

    kern = functools.partial(_conv_kernel, tiles_per_seq=tiles_per_seq)
    vec = lambda a: a.reshape(depth, 1, CV_WIDTH)
    vspec = pl.BlockSpec((None, 1, CV_WIDTH), lambda i: (layer, 0, 0))
    return pl.pallas_call(
        kern,
        grid=(t // tr,),
        in_specs=[
            pl.BlockSpec((halo, CV_WIDTH), lambda i: (prev(i), ca)),
            pl.BlockSpec((tr, CV_WIDTH), lambda i: (i, ca)),
            pl.BlockSpec((halo, CV_WIDTH), lambda i: (nxt(i), ca)),
            pl.BlockSpec((halo, CV_WIDTH), lambda i: (prev(i), cb)),
            pl.BlockSpec((tr, CV_WIDTH), lambda i: (i, cb)),
            pl.BlockSpec((halo, CV_WIDTH), lambda i: (nxt(i), cb)),
            pl.BlockSpec((tr, CV_WIDTH), lambda i: (i, cg)),
            pl.BlockSpec((None, CONV_TAPS, CV_WIDTH), lambda i: (layer, 0, 0)),
            vspec, vspec, vspec,
            pl.BlockSpec(memory_space=pl.ANY),
        ],
        out_specs=pl.BlockSpec((tr, CV_WIDTH), lambda i: (i, (ATT_WIDTH + HG_WIDTH) // CV_WIDTH)),
        out_shape=jax.ShapeDtypeStruct(mix.shape, mix.dtype),
        input_output_aliases={11: 0},
        scratch_shapes=[pltpu.VMEM((tr + 2 * halo, CV_WIDTH), F32), pltpu.VMEM((tr, CV_WIDTH), F32)],
        compiler_params=_cparams(("parallel",)),
        name="conv",
    )(z, z, z, z, z, z, z, conv_w, vec(conv_b), vec(ln_w), vec(ln_b), mix)


def kernel(x_prompt, x_sample, cache_k, cache_v, state_hgrn, c, c_ctx, ada_w, ada_b, w_in, w_out,
           attn_sink, hg_lower_bounds, hg_norm_w, conv_w, conv_b, conv_ln_w, conv_ln_b, ln_w, ln_b):
    batch, seq, d = x_prompt.shape
    dec_batch, dec_seq, _ = x_sample.shape
    depth = w_in.shape[0]
    past = cache_k.shape[2]
    alpha = float((2 * depth) ** 0.25)
    assert dec_batch + 1 <= MOD_ROWS and w_in.shape[2] == D_IN

    conds = jnp.concatenate([c_ctx[None, :], c, jnp.zeros((MOD_ROWS - 1 - dec_batch, d), F32)], axis=0)
    mod = _modulation(conds, ada_w, ada_b)
    w_in_bf = w_in.astype(BF16)
    w_out_bf = w_out.astype(BF16)
    cos, sin = _rope_tables(dec_seq)
    ck = cache_k.reshape(dec_batch, depth, past, KV_WIDTH)
    cv = cache_v.reshape(dec_batch, depth, past, KV_WIDTH)

    y = x_prompt.reshape(batch * seq, d)
    caches = states = None
    for l in range(depth):
        z, caches = _in_proj(y, mod, w_in_bf, l, 0, seq, caches, want_cache=True)
        mix = _ctx_attention(z, attn_sink, l, batch, seq, d)
        mix, states = _hgrn(z, mix, hg_lower_bounds, hg_norm_w, None, l, batch, seq, True, states)
        mix = _conv(z, mix, conv_w, conv_b, conv_ln_w, conv_ln_b, l, seq)
        y = _out_proj(mix, y, mod, w_out_bf, ln_w, ln_b, l, 0, seq, alpha)
    y_prompt = y.reshape(batch, seq, d)
    new_cache_k = caches[0].reshape(batch, depth, seq, ATT_KV_HEADS, HEAD_DIM)
    new_cache_v = caches[1].reshape(batch, depth, seq, ATT_KV_HEADS, HEAD_DIM)

    y = x_sample.reshape(dec_batch * dec_seq, d)
    for l in range(depth):
        z, _ = _in_proj(y, mod, w_in_bf, l, 1, dec_seq)
        qk = _rope(z, cos, sin, dec_seq)
        mix = _lat_attention(z, qk, ck, cv, attn_sink, l, dec_batch, dec_seq, d)
        mix, _ = _hgrn(z, mix, hg_lower_bounds, hg_norm_w, state_hgrn, l, dec_batch, dec_seq, False)
        mix = _conv(z, mix, conv_w, conv_b, conv_ln_w, conv_ln_b, l, dec_seq)
        y = _out_proj(mix, y, mod, w_out_bf, ln_w, ln_b, l, 1, dec_seq, alpha)
    y_sample = y.reshape(dec_batch, dec_seq, d)

    return (y_prompt, y_sample, new_cache_k, new_cache_v, states)
```
